```python
import math
import jax, jax.numpy as jnp
from jax import lax
import numpy as np

D_MODEL = 1024
BATCH = 8
SEQ = 2048
DEPTH = 2
DEC_BATCH = 128
DEC_SEQ = 1
PAST_LEN = 2048
PAGE_SIZE = 128

N_SB_LAYERS = (DEPTH + 1) // 2
N_RET_LAYERS = DEPTH // 2
SB_HEADS = 8
SB_HEAD_DIM = 64
SB_WIDTH = SB_HEADS * SB_HEAD_DIM
QBLOCK = 128
SB_BIAS_INIT = -6.0
POOL_WINDOWS = (2, 4, 8, 16)
POOL_GROUP = 128
POOL_WIDTH = len(POOL_WINDOWS) * POOL_GROUP
POOL_BUF = max(POOL_WINDOWS) - 1
EVEN_IN = 3 * SB_WIDTH + POOL_WIDTH
EVEN_MIX = SB_WIDTH + POOL_WIDTH
RET_HEADS = 4
RET_QK_DIM = 256
RET_V_DIM = 512
RET_QK_WIDTH = RET_HEADS * RET_QK_DIM
RET_V_WIDTH = RET_HEADS * RET_V_DIM
ODD_IN = 2 * RET_QK_WIDTH + 2 * RET_V_WIDTH
RET_CHUNK = 128
ROPE_BASE = 10000.0
D_FF = 2816
N_SUB = 3
NORM_EPS = 1e-6

kernel_name = 'hybrid_stickbreak_pool_retention_step'


def rmsnorm(x, g):
    xf = x.astype(jnp.float32)
    y = xf * lax.rsqrt(jnp.mean(xf * xf, axis=-1, keepdims=True) + NORM_EPS)
    return (y * g.astype(jnp.float32)).astype(x.dtype)


def adaln_pre(x, m, g):
    return rmsnorm(x, g) * (1 + m[:, 1][:, None]) + m[:, 0][:, None]


def adaln_post(x, y, m, g, res_w):
    return x + res_w * m[:, 2][:, None] * rmsnorm(y, g)


def swiglu(h, w_in, w_out):
    gt, up = jnp.split(h @ w_in, 2, axis=-1)
    return (jax.nn.silu(gt) * up) @ w_out


def rotary(x, pos):
    half = x.shape[-1] // 2
    inv = ROPE_BASE ** (-jnp.arange(half, dtype=jnp.float32) / half)
    ang = pos[:, None] * inv[None, :]
    cos = jnp.cos(ang)[None, :, None, :]
    sin = jnp.sin(ang)[None, :, None, :]
    xf = x.astype(jnp.float32)
    x1, x2 = xf[..., :half], xf[..., half:]
    return jnp.concatenate([x1 * cos - x2 * sin, x1 * sin + x2 * cos], axis=-1).astype(x.dtype)


def stick_breaking(q, k, v, q_pos0, sb_bias):
    T = q.shape[1]
    scale = SB_HEAD_DIM ** -0.5
    bias = sb_bias.astype(jnp.float32)[None, :, None, None]
    outs = []
    for qs in range(0, T, QBLOCK):
        qe = min(qs + QBLOCK, T)
        kend = max(q_pos0 + qe - 1, 1)
        qb, kb, vb = q[:, qs:qe], k[:, :kend], v[:, :kend]
        q_pos = q_pos0 + jnp.arange(qs, qe)
        k_pos = jnp.arange(kend)
        mask = k_pos[None, :] < q_pos[:, None]
        z = jnp.einsum('bqhd,bkhd->bhqk', qb, kb).astype(jnp.float32) * scale + bias
        log_1m = jnp.where(mask, jax.nn.log_sigmoid(-z), 0.0)
        log_w = jax.nn.log_sigmoid(z) + lax.cumsum(log_1m, axis=3, reverse=True) - log_1m
        w = jnp.where(mask, jnp.exp(log_w), 0.0)
        outs.append(jnp.einsum('bhqk,bkhd->bqhd', w.astype(vb.dtype), vb))
    return jnp.concatenate(outs, axis=1)


def multiscale_pool(u_ext, n_new, pool_w, pool_scale):
    B, PT, C = u_ext.shape
    n_prev = PT - n_new
    cs = jnp.concatenate([jnp.zeros((B, 1, C), jnp.float32),
                          jnp.cumsum(u_ext.astype(jnp.float32), axis=1)], axis=1)
    end = n_prev + 1 + jnp.arange(n_new)
    cs_end = cs[:, end]
    means = []
    for gi, w in enumerate(POOL_WINDOWS):
        start = jnp.maximum(end - w, 0)
        cnt = (end - start).astype(jnp.float32)[None, :, None]
        lo, hi = gi * POOL_GROUP, (gi + 1) * POOL_GROUP
        means.append((cs_end[..., lo:hi] - cs[:, start, lo:hi]) / cnt)
    mean = jnp.concatenate(means, axis=-1)
    d = (mean - u_ext[:, n_prev:].astype(jnp.float32)).astype(u_ext.dtype)
    d = d.reshape(B, n_new, len(POOL_WINDOWS), POOL_GROUP)
    y = jnp.einsum('btgc,gcd->btgd', d, pool_w).reshape(B, n_new, POOL_WIDTH)
    return y * pool_scale


def even_mixer(h, w_in, sb_bias, pool_w, pool_scale, w_out, k_past, v_past, pool_buf):
    B, T, _ = h.shape
    q, k, v, u = jnp.split(h @ w_in, [SB_WIDTH, 2 * SB_WIDTH, 3 * SB_WIDTH], axis=-1)
    q = q.reshape(B, T, SB_HEADS, SB_HEAD_DIM)
    k = k.reshape(B, T, SB_HEADS, SB_HEAD_DIM)
    v = v.reshape(B, T, SB_HEADS, SB_HEAD_DIM)
    if k_past is None:
        k_all, v_all, u_ext, pos0 = k, v, u, 0
    else:
        k_all = jnp.concatenate([k_past.astype(k.dtype), k], axis=1)
        v_all = jnp.concatenate([v_past.astype(v.dtype), v], axis=1)
        u_ext = jnp.concatenate([pool_buf.astype(u.dtype), u], axis=1)
        pos0 = k_past.shape[1]
    o_sb = stick_breaking(q, k_all, v_all, pos0, sb_bias).reshape(B, T, SB_WIDTH)
    o_pool = multiscale_pool(u_ext, T, pool_w, pool_scale)
    y = jnp.concatenate([o_sb, o_pool], axis=-1) @ w_out
    return y, k, v, u_ext[:, -POOL_BUF:]


def retention_chunk(S, q, k, v, log_gamma):
    L = q.shape[1]
    i = jnp.arange(L, dtype=jnp.float32)
    diff = i[:, None] - i[None, :]
    decay = jnp.where(diff[None] >= 0,
                      jnp.exp(jnp.maximum(diff, 0.0)[None] * log_gamma[:, None, None]), 0.0)
    scores = jnp.einsum('blhd,bmhd->bhlm', q, k) * decay[None]
    intra = jnp.einsum('bhlm,bmhe->blhe', scores, v)
    cross = jnp.einsum('blhd,bhde->blhe', q, S) * jnp.exp((i[:, None] + 1.0) * log_gamma[None, :])[None, :, :, None]
    k_dec = k * jnp.exp((L - 1.0 - i)[:, None] * log_gamma[None, :])[None, :, :, None]
    S_new = jnp.exp(L * log_gamma)[None, :, None, None] * S + jnp.einsum('blhd,blhe->bhde', k_dec, v)
    return S_new, intra + cross


def retention_mixer(h, w_in, w_out, S0, pos0):
    B, T, _ = h.shape
    q, k, v, g = jnp.split(h @ w_in, [RET_QK_WIDTH, 2 * RET_QK_WIDTH, 2 * RET_QK_WIDTH + RET_V_WIDTH], axis=-1)
    pos = pos0 + jnp.arange(T, dtype=jnp.float32)
    q = rotary(q.reshape(B, T, RET_HEADS, RET_QK_DIM), pos).astype(jnp.float32)
    k = rotary(k.reshape(B, T, RET_HEADS, RET_QK_DIM), pos).astype(jnp.float32) * (RET_QK_DIM ** -0.5)
    v = v.reshape(B, T, RET_HEADS, RET_V_DIM).astype(jnp.float32)
    log_gamma = jnp.log(1.0 - jnp.exp2(-5.0 - jnp.arange(RET_HEADS, dtype=jnp.float32)))
    if S0 is None:
        S0 = jnp.zeros((B, RET_HEADS, RET_QK_DIM, RET_V_DIM), jnp.float32)
    else:
        S0 = S0.astype(jnp.float32)
    if T % RET_CHUNK == 0:
        nc = T // RET_CHUNK
        xs = tuple(jnp.moveaxis(a.reshape(B, nc, RET_CHUNK, a.shape[2], a.shape[3]), 1, 0) for a in (q, k, v))
        S, o = lax.scan(lambda s, inp: retention_chunk(s, inp[0], inp[1], inp[2], log_gamma), S0, xs)
        o = jnp.moveaxis(o, 0, 1).reshape(B, T, RET_HEADS, RET_V_DIM)
    else:
        S, o = retention_chunk(S0, q, k, v, log_gamma)
    mu = jnp.mean(o, axis=-1, keepdims=True)
    var = jnp.mean(jnp.square(o - mu), axis=-1, keepdims=True)
    o = ((o - mu) * lax.rsqrt(var + 1e-5)).reshape(B, T, RET_V_WIDTH).astype(h.dtype)
    return (o * jax.nn.silu(g)) @ w_out, S


def run_trunk(x, c, sb_past, pool_past, ret_past, pos0, params):
    (w_ada, b_ada, norm_pre, norm_post, ffn_w_in, ffn_w_out,
     w_in_even, sb_bias, pool_w, pool_scale, w_out_even, w_in_odd, w_out_odd) = params
    B = x.shape[0]
    cond = jax.nn.silu(c)
    new_k, new_v, new_pool, new_ret = [], [], [], []
    for l in range(DEPTH):
        mod = (cond @ w_ada[l] + b_ada[l]).reshape(B, N_SUB, 3, D_MODEL)
        h = adaln_pre(x, mod[:, 0], norm_pre[l, 0])
        x = adaln_post(x, swiglu(h, ffn_w_in[l, 0], ffn_w_out[l, 0]), mod[:, 0], norm_post[l, 0], 0.5)
        h = adaln_pre(x, mod[:, 1], norm_pre[l, 1])
        li = l // 2
        if l % 2 == 0:
            kp = None if sb_past is None else sb_past[li][0]
            vp = None if sb_past is None else sb_past[li][1]
            pb = None if pool_past is None else pool_past[li]
            y, k_new, v_new, buf = even_mixer(h, w_in_even[li], sb_bias[li], pool_w[li], pool_scale[li],
                                              w_out_even[li], kp, vp, pb)
            new_k.append(k_new)
            new_v.append(v_new)
            new_pool.append(buf)
        else:
            s0 = None if ret_past is None else ret_past[li]
            y, S = retention_mixer(h, w_in_odd[li], w_out_odd[li], s0, pos0)
            new_ret.append(S)
        x = adaln_post(x, y, mod[:, 1], norm_post[l, 1], 1.0)
        h = adaln_pre(x, mod[:, 2], norm_pre[l, 2])
        x = adaln_post(x, swiglu(h, ffn_w_in[l, 1], ffn_w_out[l, 1]), mod[:, 2], norm_post[l, 2], 0.5)
    return x, new_k, new_v, new_pool, new_ret


def setup_inputs(seed: int = 0) -> dict:
    key = jax.random.key(seed)
    ks = jax.random.split(key, 22)
    f32 = jnp.float32
    n_pages = PAST_LEN // PAGE_SIZE
    n_used = DEC_BATCH * n_pages
    n_phys = n_used + max(1, n_used // 4)
    nrm = lambda k, shape, s: jax.random.normal(k, shape, f32) * s
    page_table = jax.random.permutation(ks[8], n_phys)[:n_used].reshape(DEC_BATCH, n_pages).astype(jnp.int32)
    return {
        'x_prompt': nrm(ks[0], (BATCH, SEQ, D_MODEL), 1.0),
        'x_sample': nrm(ks[1], (DEC_BATCH, DEC_SEQ, D_MODEL), 1.0),
        'c_prompt': nrm(ks[2], (BATCH, D_MODEL), 1.0),
        'c_sample': nrm(ks[3], (DEC_BATCH, D_MODEL), 1.0),
        'cache_k': nrm(ks[4], (N_SB_LAYERS, n_phys, PAGE_SIZE, SB_HEADS, SB_HEAD_DIM), 1.0),
        'cache_v': nrm(ks[5], (N_SB_LAYERS, n_phys, PAGE_SIZE, SB_HEADS, SB_HEAD_DIM), 1.0),
        'state_pool': nrm(ks[6], (N_SB_LAYERS, DEC_BATCH, POOL_BUF, POOL_WIDTH), 1.0),
        'state_ret': nrm(ks[7], (N_RET_LAYERS, DEC_BATCH, RET_HEADS, RET_QK_DIM, RET_V_DIM), 0.5),
        'page_table': page_table,
        'w_ada': nrm(ks[9], (DEPTH, D_MODEL, N_SUB * 3 * D_MODEL), 0.5 * D_MODEL ** -0.5),
        'b_ada': nrm(ks[10], (DEPTH, N_SUB * 3 * D_MODEL), 0.01),
        'norm_pre': 1.0 + nrm(ks[11], (DEPTH, N_SUB, D_MODEL), 0.05),
        'norm_post': 1.0 + nrm(ks[12], (DEPTH, N_SUB, D_MODEL), 0.05),
        'ffn_w_in': nrm(ks[13], (DEPTH, 2, D_MODEL, 2 * D_FF), D_MODEL ** -0.5),
        'ffn_w_out': nrm(ks[14], (DEPTH, 2, D_FF, D_MODEL), D_FF ** -0.5),
        'w_in_even': nrm(ks[15], (N_SB_LAYERS, D_MODEL, EVEN_IN), D_MODEL ** -0.5),
        'sb_bias': SB_BIAS_INIT + nrm(ks[21], (N_SB_LAYERS, SB_HEADS), 0.1),
        'pool_w': nrm(ks[16], (N_SB_LAYERS, len(POOL_WINDOWS), POOL_GROUP, POOL_GROUP), POOL_GROUP ** -0.5),
        'pool_scale': 1.0 + nrm(ks[17], (N_SB_LAYERS, POOL_WIDTH), 0.1),
        'w_out_even': nrm(ks[18], (N_SB_LAYERS, EVEN_MIX, D_MODEL), EVEN_MIX ** -0.5),
        'w_in_odd': nrm(ks[19], (N_RET_LAYERS, D_MODEL, ODD_IN), D_MODEL ** -0.5),
        'w_out_odd': nrm(ks[20], (N_RET_LAYERS, RET_V_WIDTH, D_MODEL), RET_V_WIDTH ** -0.5),
    }


def reference(x_prompt, x_sample, c_prompt, c_sample, cache_k, cache_v, state_pool, state_ret,
              page_table, w_ada, b_ada, norm_pre, norm_post, ffn_w_in, ffn_w_out,
              w_in_even, sb_bias, pool_w, pool_scale, w_out_even, w_in_odd, w_out_odd):
    params = (w_ada, b_ada, norm_pre, norm_post, ffn_w_in, ffn_w_out,
              w_in_even, sb_bias, pool_w, pool_scale, w_out_even, w_in_odd, w_out_odd)
    db, n_pages = page_table.shape
    past_len = n_pages * cache_k.shape[2]
    y_prompt, kp, vp, pp, rp = run_trunk(x_prompt, c_prompt, None, None, None, 0, params)
    sb_past = []
    for li in range(N_SB_LAYERS):
        kg = cache_k[li][page_table].reshape(db, past_len, SB_HEADS, SB_HEAD_DIM)
        vg = cache_v[li][page_table].reshape(db, past_len, SB_HEADS, SB_HEAD_DIM)
        sb_past.append((kg, vg))
    pool_past = [state_pool[li] for li in range(N_SB_LAYERS)]
    ret_past = [state_ret[li] for li in range(N_RET_LAYERS)]
    y_sample, ks_, vs_, ps_, rs_ = run_trunk(x_sample, c_sample, sb_past, pool_past, ret_past, past_len, params)
    return (y_prompt, y_sample,
            jnp.stack(kp, 0), jnp.stack(vp, 0), jnp.stack(pp, 0), jnp.stack(rp, 0),
            jnp.stack(ks_, 0), jnp.stack(vs_, 0), jnp.stack(ps_, 0), jnp.stack(rs_, 0))
```

```python
import functools

import numpy as np
import jax
import jax.numpy as jnp
from jax import lax
from jax.experimental import pallas as pl
from jax.experimental.pallas import tpu as pltpu

F32 = jnp.float32
BF16 = jnp.bfloat16

D_MODEL = 1024
D_FF = 2816
N_SUB = 3
NORM_EPS = 1e-6
SB_HEADS = 8
SB_HEAD_DIM = 64
SB_WIDTH = SB_HEADS * SB_HEAD_DIM
N_PAIRS = SB_HEADS // 2
POOL_WINDOWS = (2, 4, 8, 16)
POOL_GROUP = 128
POOL_WIDTH = len(POOL_WINDOWS) * POOL_GROUP
POOL_HIST = 16
RET_HEADS = 4
RET_QK_DIM = 256
RET_V_DIM = 512
RET_QK_WIDTH = RET_HEADS * RET_QK_DIM
RET_V_WIDTH = RET_HEADS * RET_V_DIM
RET_CHUNK = 128
ROPE_BASE = 10000.0
LN_EPS = 1e-5
KBLK = 128
VMEM_LIMIT_BYTES = 56 * 1024 * 1024


def _params(*sem):
    return pltpu.CompilerParams(dimension_semantics=sem, vmem_limit_bytes=VMEM_LIMIT_BYTES)


def _resident(shape):
    nd = len(shape)
    return pl.BlockSpec(shape, lambda *_: (0,) * nd, pipeline_mode=pl.Buffered(1))


def _rms(x, g):
    ms = jnp.mean(x * x, axis=-1, keepdims=True)
    return x * lax.rsqrt(ms + NORM_EPS) * g


def _adaln_pre(x, m_ref, g):
    return _rms(x, g) * (1.0 + m_ref[0, 1]) + m_ref[0, 0]


def _silu(x):
    return x * jax.nn.sigmoid(x)


def _dot(a, b):
    return jnp.dot(a, b, preferred_element_type=F32)


def _dot_nt(a, b):
    return lax.dot_general(a, b, (((1,), (1,)), ((), ())), preferred_element_type=F32)


def _dot_tn(a, b):
    return lax.dot_general(a, b, (((0,), (0,)), ((), ())), preferred_element_type=F32)


def _mod_kernel(cp_ref, cs_ref, w_ref, b_ref, mp_ref, ms_ref):
    w = w_ref[0].astype(BF16)
    b = b_ref[0, 0]
    mp_ref[0, 0] = _dot(_silu(cp_ref[...]).astype(BF16), w) + b
    ms_ref[0, 0] = _dot(_silu(cs_ref[...]).astype(BF16), w) + b


def _modulation(c_p, c_s, w_ada, b_ada):
    depth = w_ada.shape[0]
    nb = N_SUB * 3
    bp, bs = c_p.shape[0], c_s.shape[0]
    return pl.pallas_call(
        _mod_kernel,
        grid=(depth, nb),
        in_specs=[
            pl.BlockSpec((bp, D_MODEL), lambda l, n: (0, 0)),
            pl.BlockSpec((bs, D_MODEL), lambda l, n: (0, 0)),
            pl.BlockSpec((1, D_MODEL, D_MODEL), lambda l, n: (l, 0, n)),
            pl.BlockSpec((1, 1, 1, D_MODEL), lambda l, n: (l, n, 0, 0)),
        ],
        out_specs=[
            pl.BlockSpec((1, 1, bp, D_MODEL), lambda l, n: (l, n, 0, 0)),
            pl.BlockSpec((1, 1, bs, D_MODEL), lambda l, n: (l, n, 0, 0)),
        ],
        out_shape=[
            jax.ShapeDtypeStruct((depth, nb, bp, D_MODEL), F32),
            jax.ShapeDtypeStruct((depth, nb, bs, D_MODEL), F32),
        ],
        compiler_params=_params("arbitrary", "arbitrary"),
        name="adaln_modulation",
    )(c_p, c_s, w_ada, b_ada.reshape(depth, nb, 1, D_MODEL))


def _ffn_kernel(x_ref, m_ref, gpre_ref, gpost_ref, win_ref, wout_ref, o_ref):
    x = x_ref[0]
    hb = _adaln_pre(x, m_ref, gpre_ref[...]).astype(BF16)
    gu = _dot(hb, win_ref[...])
    a = (_silu(gu[:, :D_FF]) * gu[:, D_FF:]).astype(BF16)
    y = _dot(a, wout_ref[...])
    o_ref[0] = x + 0.5 * m_ref[0, 2] * _rms(y, gpost_ref[...])


def _ffn_sublayer(x, mod, g_pre, g_post, w_in, w_out, tm):
    g, t, _ = x.shape
    r = mod.shape[2]
    rb = 1 if r == 1 else tm
    return pl.pallas_call(
        _ffn_kernel,
        grid=(g, t // tm),
        in_specs=[
            pl.BlockSpec((1, tm, D_MODEL), lambda b, j: (b, j, 0)),
            pl.BlockSpec((1, 3, rb, D_MODEL), (lambda b, j: (b, 0, 0, 0)) if r == 1 else (lambda b, j: (b, 0, j, 0))),
            _resident((1, D_MODEL)),
            _resident((1, D_MODEL)),
            _resident((D_MODEL, 2 * D_FF)),
            _resident((D_FF, D_MODEL)),
        ],
        out_specs=pl.BlockSpec((1, tm, D_MODEL), lambda b, j: (b, j, 0)),
        out_shape=jax.ShapeDtypeStruct(x.shape, F32),
        compiler_params=_params("arbitrary", "arbitrary"),
        name="ffn_sublayer",
    )(x, mod, g_pre, g_post, w_in, w_out)


def _softplus(z):
    return jnp.maximum(z, 0.0) + jnp.log1p(jnp.exp(-jnp.abs(z)))


def _even_prompt_kernel(bias_ref, x_ref, m_ref, gpre_ref, gpost_ref, win_ref, pw_ref, ps_ref, wout_ref,
                        xo_ref, k_ref, v_ref, tail_ref,
                        kbuf, vtbuf, ubuf, obuf, acc_ref, car_ref, *, tm, tq):
    j = pl.program_id(1)
    nj = pl.num_programs(1)
    x = x_ref[0]
    hb = _adaln_pre(x, m_ref, gpre_ref[...]).astype(BF16)
    qkvu = _dot(hb, win_ref[...])
    q = qkvu[:, 0:SB_WIDTH] * (SB_HEAD_DIM ** -0.5)
    k = qkvu[:, SB_WIDTH:2 * SB_WIDTH]
    v = qkvu[:, 2 * SB_WIDTH:3 * SB_WIDTH]
    u = qkvu[:, 3 * SB_WIDTH:]
    k_ref[0] = k
    v_ref[0] = v
    nsub = tm // KBLK
    for s in range(nsub):
        kbuf[j * nsub + s] = k[s * KBLK:(s + 1) * KBLK].astype(BF16)
        vtbuf[j * nsub + s] = v[s * KBLK:(s + 1) * KBLK].T.astype(BF16)

    lane = lax.broadcasted_iota(jnp.int32, (tq, 2 * SB_HEAD_DIM), 1)
    krow = lax.broadcasted_iota(jnp.int32, (KBLK, 2 * tq), 0)
    qcol = lax.broadcasted_iota(jnp.int32, (KBLK, 2 * tq), 1)
    qcol = jnp.where(qcol >= tq, qcol - tq, qcol)
    first_head = lax.broadcasted_iota(jnp.int32, (1, 2 * tq), 1) < tq
    mr = lax.broadcasted_iota(jnp.int32, (KBLK, 2 * KBLK), 0)
    mc = lax.broadcasted_iota(jnp.int32, (KBLK, 2 * KBLK), 1)
    mc = jnp.where(mc >= KBLK, mc - KBLK, mc)
    suffix_neg = jnp.where(mc > mr, -1.0, 0.0).astype(BF16)

    for qt in range(tm // tq):
        qtile = j * (tm // tq) + qt
        qrows = q[qt * tq:(qt + 1) * tq]
        qs_t, bias_row = [], []
        for p in range(N_PAIRS):
            q2 = qrows[:, 128 * p:128 * (p + 1)]
            qa = jnp.where(lane < SB_HEAD_DIM, q2, 0.0).T
            qb = jnp.where(lane >= SB_HEAD_DIM, q2, 0.0).T
            qs_t.append(jnp.concatenate([qa, qb], axis=1).astype(BF16))
            bias_row.append(jnp.where(first_head, bias_ref[2 * p], bias_ref[2 * p + 1]))
        acc_ref[...] = jnp.zeros_like(acc_ref)
        car_ref[...] = jnp.zeros_like(car_ref)

        def step(kb, masked):
            kblk = kbuf[kb]
            vtb = vtbuf[kb]
            if masked:
                valid = (kb * KBLK + krow) < (qtile * tq + qcol)
            for p in range(N_PAIRS):
                z = _dot(kblk[:, 128 * p:128 * (p + 1)], qs_t[p]) + bias_row[p]
                sp = _softplus(z)
                logb = z - sp
                if masked:
                    sp = jnp.where(valid, sp, 0.0)
                hi = sp.astype(BF16)
                lo = (sp - hi.astype(F32)).astype(BF16)
                excl = _dot(suffix_neg, jnp.concatenate([hi, lo], axis=0))
                w = jnp.exp(logb + excl + car_ref[p])
                if masked:
                    w = jnp.where(valid, w, 0.0)
                acc_ref[p] += _dot(vtb[128 * p:128 * (p + 1), :], w.astype(BF16))
                car_ref[p] -= jnp.sum(sp, axis=0, keepdims=True)

        ndiag = tq // KBLK
        n_full = qtile * ndiag
        for dk in range(ndiag - 1, -1, -1):
            step(n_full + dk, True)

        def body(it, carry):
            step(n_full - 1 - it, False)
            return carry

        lax.fori_loop(0, n_full, body, 0)

        for p in range(N_PAIRS):
            a = acc_ref[p]
            o_t = jnp.concatenate([a[0:SB_HEAD_DIM, 0:tq], a[SB_HEAD_DIM:, tq:]], axis=0)
            obuf[qt * tq:(qt + 1) * tq, 128 * p:128 * (p + 1)] = o_t.T.astype(BF16)

    @pl.when(j == 0)
    def _():
        ubuf[0:POOL_HIST] = jnp.zeros((POOL_HIST, POOL_WIDTH), F32)

    ubuf[POOL_HIST:POOL_HIST + tm] = u
    pos1 = (j * tm + 1 + lax.broadcasted_iota(jnp.int32, (tm, POOL_GROUP), 0)).astype(F32)
    for gi, wdw in enumerate(POOL_WINDOWS):
        lo_, hi_ = gi * POOL_GROUP, (gi + 1) * POOL_GROUP
        ug = u[:, lo_:hi_]
        ssum = ug
        for sft in range(1, wdw):
            ssum = ssum + ubuf[POOL_HIST - sft:POOL_HIST - sft + tm, lo_:hi_]
        d = ssum / jnp.minimum(pos1, float(wdw)) - ug
        yg = _dot(d.astype(BF16), pw_ref[gi]) * ps_ref[:, lo_:hi_]
        obuf[:, SB_WIDTH + lo_:SB_WIDTH + hi_] = yg.astype(BF16)
    hist = ubuf[tm:tm + POOL_HIST]
    ubuf[0:POOL_HIST] = hist

    @pl.when(j == nj - 1)
    def _():
        tail_ref[0] = hist

    y = _dot(obuf[...], wout_ref[...])
    xo_ref[0] = x + m_ref[0, 2] * _rms(y, gpost_ref[...])


def _even_prompt(x, mod, g_pre, g_post, w_in, sb_bias, pool_w, pool_scale, w_out, tm, tq):
    b, t, _ = x.shape
    nkb = t // KBLK
    grid_spec = pltpu.PrefetchScalarGridSpec(
        num_scalar_prefetch=1,
        grid=(b, t // tm),
        in_specs=[
            pl.BlockSpec((1, tm, D_MODEL), lambda i, j, *_: (i, j, 0)),
            pl.BlockSpec((1, 3, 1, D_MODEL), lambda i, j, *_: (i, 0, 0, 0)),
            _resident((1, D_MODEL)),
            _resident((1, D_MODEL)),
            _resident(w_in.shape),
            _resident(pool_w.shape),
            _resident((1, POOL_WIDTH)),
            _resident(w_out.shape),
        ],
        out_specs=[
            pl.BlockSpec((1, tm, D_MODEL), lambda i, j, *_: (i, j, 0)),
            pl.BlockSpec((1, tm, SB_WIDTH), lambda i, j, *_: (i, j, 0)),
            pl.BlockSpec((1, tm, SB_WIDTH), lambda i, j, *_: (i, j, 0)),
            pl.BlockSpec((1, POOL_HIST, POOL_WIDTH), lambda i, j, *_: (i, 0, 0)),
        ],
        scratch_shapes=[
            pltpu.VMEM((nkb, KBLK, SB_WIDTH), BF16),
            pltpu.VMEM((nkb, SB_WIDTH, KBLK), BF16),
            pltpu.VMEM((POOL_HIST + tm, POOL_WIDTH), F32),
            pltpu.VMEM((tm, SB_WIDTH + POOL_WIDTH), BF16),
            pltpu.VMEM((N_PAIRS, 2 * SB_HEAD_DIM, 2 * tq), F32),
            pltpu.VMEM((N_PAIRS, 1, 2 * tq), F32),
        ],
    )
    return pl.pallas_call(
        functools.partial(_even_prompt_kernel, tm=tm, tq=tq),
        grid_spec=grid_spec,
        out_shape=[
            jax.ShapeDtypeStruct(x.shape, F32),
            jax.ShapeDtypeStruct((b, t, SB_WIDTH), F32),
            jax.ShapeDtypeStruct((b, t, SB_WIDTH), F32),
            jax.ShapeDtypeStruct((b, POOL_HIST, POOL_WIDTH), F32),
        ],
        compiler_params=_params("arbitrary", "arbitrary"),
        name="even_prompt",
    )(sb_bias, x, mod, g_pre, g_post, w_in, pool_w, pool_scale, w_out)


def _ret_log_gamma():
    return np.log(np.float32(1.0) - np.exp2(np.float32(-5.0) - np.arange(RET_HEADS, dtype=np.float32))).astype(np.float32)


def _ret_tables(chunk):
    lg = jnp.asarray(_ret_log_gamma())
    i = jnp.arange(chunk, dtype=F32)
    diff = i[:, None] - i[None, :]
    decay = jnp.where(diff[None] >= 0, jnp.exp(jnp.maximum(diff, 0.0)[None] * lg[:, None, None]), 0.0)
    cross = jnp.exp((i[None, :] + 1.0) * lg[:, None])[:, :, None]
    kdec = jnp.exp((chunk - 1.0 - i)[None, :] * lg[:, None])[:, :, None]
    full = jnp.exp(chunk * lg)
    return decay, cross, kdec, full


def _rope_tables(pos):
    half = RET_QK_DIM // 2
    inv = ROPE_BASE ** (-jnp.arange(half, dtype=F32) / half)
    ang = pos[:, None] * inv[None, :]
    return jnp.cos(ang), jnp.sin(ang)


def _rotate(x, cos, sin):
    half = RET_QK_DIM // 2
    x1, x2 = x[:, :half], x[:, half:]
    return jnp.concatenate([x1 * cos - x2 * sin, x1 * sin + x2 * cos], axis=-1)


def _group_norm_gate(o, g):
    mu = jnp.mean(o, axis=-1, keepdims=True)
    c = o - mu
    var = jnp.mean(c * c, axis=-1, keepdims=True)
    return c * lax.rsqrt(var + LN_EPS) * _silu(g)


def _odd_prompt_kernel(full_ref, x_ref, m_ref, gpre_ref, gpost_ref, cos_ref, sin_ref, win_ref, wout_ref,
                       dec_ref, crs_ref, kdc_ref, xo_ref, s_ref, obuf, *, tm):
    j = pl.program_id(1)

    @pl.when(j == 0)
    def _():
        s_ref[...] = jnp.zeros_like(s_ref)

    x = x_ref[0]
    hb = _adaln_pre(x, m_ref, gpre_ref[...]).astype(BF16)
    cos, sin = cos_ref[...], sin_ref[...]
    for hd in range(RET_HEADS):
        qo = hd * RET_QK_DIM
        ko = RET_QK_WIDTH + hd * RET_QK_DIM
        vo = 2 * RET_QK_WIDTH + hd * RET_V_DIM
        go = 2 * RET_QK_WIDTH + RET_V_WIDTH + hd * RET_V_DIM
        qr = _rotate(_dot(hb, win_ref[:, qo:qo + RET_QK_DIM]), cos, sin)
        kr = _rotate(_dot(hb, win_ref[:, ko:ko + RET_QK_DIM]), cos, sin) * (RET_QK_DIM ** -0.5)
        vh = _dot(hb, win_ref[:, vo:vo + RET_V_DIM]).astype(BF16)
        gh = _dot(hb, win_ref[:, go:go + RET_V_DIM])
        for c in range(tm // RET_CHUNK):
            rows = slice(c * RET_CHUNK, (c + 1) * RET_CHUNK)
            qc = qr[rows].astype(BF16)
            kc = kr[rows]
            vc = vh[rows]
            scores = _dot_nt(qc, kc.astype(BF16)) * dec_ref[hd]
            state = s_ref[0, hd]
            o = _dot(scores.astype(BF16), vc) + _dot(qc, state.astype(BF16)) * crs_ref[hd]
            kd = (kc * kdc_ref[hd]).astype(BF16)
            s_ref[0, hd] = full_ref[hd] * state + _dot_tn(kd, vc)
            obuf[rows, hd * RET_V_DIM:(hd + 1) * RET_V_DIM] = _group_norm_gate(o, gh[rows]).astype(BF16)
    y = _dot(obuf[...], wout_ref[...])
    xo_ref[0] = x + m_ref[0, 2] * _rms(y, gpost_ref[...])


def _odd_prompt(x, mod, g_pre, g_post, w_in, w_out, tm):
    b, t, _ = x.shape
    cos, sin = _rope_tables(jnp.arange(t, dtype=F32))
    decay, cross, kdec, full = _ret_tables(RET_CHUNK)
    half = RET_QK_DIM // 2
    grid_spec = pltpu.PrefetchScalarGridSpec(
        num_scalar_prefetch=1,
        grid=(b, t // tm),
        in_specs=[
            pl.BlockSpec((1, tm, D_MODEL), lambda i, j, *_: (i, j, 0)),
            pl.BlockSpec((1, 3, 1, D_MODEL), lambda i, j, *_: (i, 0, 0, 0)),
            _resident((1, D_MODEL)),
            _resident((1, D_MODEL)),
            pl.BlockSpec((tm, half), lambda i, j, *_: (j, 0)),
            pl.BlockSpec((tm, half), lambda i, j, *_: (j, 0)),
            _resident(w_in.shape),
            _resident(w_out.shape),
            _resident(decay.shape),
            _resident(cross.shape),
            _resident(kdec.shape),
        ],
        out_specs=[
            pl.BlockSpec((1, tm, D_MODEL), lambda i, j, *_: (i, j, 0)),
            pl.BlockSpec((1, RET_HEADS, RET_QK_DIM, RET_V_DIM), lambda i, j, *_: (i, 0, 0, 0)),
        ],
        scratch_shapes=[pltpu.VMEM((tm, RET_V_WIDTH), BF16)],
    )
    return pl.pallas_call(
        functools.partial(_odd_prompt_kernel, tm=tm),
        grid_spec=grid_spec,
        out_shape=[
            jax.ShapeDtypeStruct(x.shape, F32),
            jax.ShapeDtypeStruct((b, RET_HEADS, RET_QK_DIM, RET_V_DIM), F32),
        ],
        compiler_params=_params("arbitrary", "arbitrary"),
        name="odd_prompt",
    )(full, x, mod, g_pre, g_post, cos, sin, w_in, w_out, decay, cross, kdec)


def _even_sample_pre_kernel(x_ref, m_ref, gpre_ref, win_ref, pool_ref, pw_ref, ps_ref,
                            q_ref, k_ref, v_ref, pool_out_ref, opool_ref):
    hb = _adaln_pre(x_ref[...], m_ref, gpre_ref[...]).astype(BF16)
    qkvu = _dot(hb, win_ref[...])
    q_ref[...] = qkvu[:, 0:SB_WIDTH] * (SB_HEAD_DIM ** -0.5)
    k_ref[...] = qkvu[:, SB_WIDTH:2 * SB_WIDTH]
    v_ref[...] = qkvu[:, 2 * SB_WIDTH:3 * SB_WIDTH]
    u = qkvu[:, 3 * SB_WIDTH:]
    nh = POOL_HIST - 1
    for gi, wdw in enumerate(POOL_WINDOWS):
        lo_, hi_ = gi * POOL_GROUP, (gi + 1) * POOL_GROUP
        ug = u[:, lo_:hi_]
        ssum = ug
        for back in range(1, wdw):
            r = nh - back
            ssum = ssum + pool_ref[:, r * POOL_WIDTH + lo_:r * POOL_WIDTH + hi_]
        d = ssum / float(wdw) - ug
        opool_ref[:, lo_:hi_] = (_dot(d.astype(BF16), pw_ref[gi]) * ps_ref[:, lo_:hi_]).astype(BF16)
    pool_out_ref[:, 0:(nh - 1) * POOL_WIDTH] = pool_ref[:, POOL_WIDTH:nh * POOL_WIDTH]
    pool_out_ref[:, (nh - 1) * POOL_WIDTH:] = u


def _even_sample_pre(x, mod, g_pre, w_in, pool_flat, pool_w, pool_scale):
    rows = x.shape[0]
    outs = [
        jax.ShapeDtypeStruct((rows, SB_WIDTH), F32),
        jax.ShapeDtypeStruct((rows, SB_WIDTH), F32),
        jax.ShapeDtypeStruct((rows, SB_WIDTH), F32),
        jax.ShapeDtypeStruct(pool_flat.shape, F32),
        jax.ShapeDtypeStruct((rows, POOL_WIDTH), BF16),
    ]
    return pl.pallas_call(
        _even_sample_pre_kernel,
        out_shape=outs,
        compiler_params=pltpu.CompilerParams(vmem_limit_bytes=VMEM_LIMIT_BYTES),
        name="even_sample_pre",
    )(x, mod, g_pre, w_in, pool_flat, pool_w, pool_scale)


def _sb_decode_kernel(pt_ref, bias_ref, q_ref, ck_ref, cv_ref, o_ref, kbuf, vbuf, sem, *, n_pages, page):
    i = pl.program_id(0)
    n = pl.num_programs(0)
    past = n_pages * page

    def copies(sample, slot):
        out = []
        for p in range(n_pages):
            pg = pt_ref[sample, p]
            out.append(pltpu.make_async_copy(ck_ref.at[pg], kbuf.at[slot, pl.ds(p * page, page)], sem.at[0, slot]))
            out.append(pltpu.make_async_copy(cv_ref.at[pg], vbuf.at[slot, pl.ds(p * page, page)], sem.at[1, slot]))
        return out

    @pl.when(i == 0)
    def _():
        for c in copies(0, 0):
            c.start()

    @pl.when(i + 1 < n)
    def _():
        for c in copies(i + 1, (i + 1) % 2):
            c.start()

    slot = i % 2
    for c in copies(i, slot):
        c.wait()

    q = q_ref[0]
    head_of_lane = lax.broadcasted_iota(jnp.int32, (SB_HEADS, SB_WIDTH), 1) // SB_HEAD_DIM
    own = head_of_lane == lax.broadcasted_iota(jnp.int32, (SB_HEADS, SB_WIDTH), 0)
    qblk = jnp.where(own, q, 0.0).astype(BF16)
    hrow = lax.broadcasted_iota(jnp.int32, (SB_HEADS, 1), 0)
    bias = jnp.zeros((SB_HEADS, 1), F32)
    for h in range(SB_HEADS):
        bias = jnp.where(hrow == h, bias_ref[h], bias)
    z = _dot_nt(qblk, kbuf[slot].astype(BF16)) + bias
    sp = _softplus(z)
    logb = z - sp
    nblk = past // KBLK
    st = jnp.concatenate([sp[:, c * KBLK:(c + 1) * KBLK] for c in range(nblk)], axis=0)
    hi = st.astype(BF16)
    lo = (st - hi.astype(F32)).astype(BF16)
    hl = jnp.concatenate([hi, lo], axis=1)
    jr = lax.broadcasted_iota(jnp.int32, (2 * KBLK, KBLK), 0)
    jr = jnp.where(jr >= KBLK, jr - KBLK, jr)
    sc = lax.broadcasted_iota(jnp.int32, (2 * KBLK, KBLK), 1)
    within = _dot(hl, jnp.where(jr > sc, -1.0, 0.0).astype(BF16))
    total = _dot(hl, jnp.full((2 * KBLK, KBLK), -1.0, BF16))
    carry = jnp.zeros((SB_HEADS, KBLK), F32)
    excl = [None] * nblk
    for c in range(nblk - 1, -1, -1):
        excl[c] = within[c * SB_HEADS:(c + 1) * SB_HEADS] + carry
        carry = carry + total[c * SB_HEADS:(c + 1) * SB_HEADS]
    w = jnp.exp(logb + jnp.concatenate(excl, axis=1))
    o8 = _dot(w.astype(BF16), vbuf[slot].astype(BF16))
    o_ref[0] = jnp.sum(jnp.where(own, o8, 0.0), axis=0, keepdims=True)


def _sb_decode(q, sb_bias, page_table, cache_k, cache_v):
    rows, n_pages = page_table.shape
    page = cache_k.shape[1]
    grid_spec = pltpu.PrefetchScalarGridSpec(
        num_scalar_prefetch=2,
        grid=(rows,),
        in_specs=[
            pl.BlockSpec((1, 1, SB_WIDTH), lambda i, *_: (i, 0, 0)),
            pl.BlockSpec(memory_space=pl.ANY),
            pl.BlockSpec(memory_space=pl.ANY),
        ],
        out_specs=pl.BlockSpec((1, 1, SB_WIDTH), lambda i, *_: (i, 0, 0)),
        scratch_shapes=[
            pltpu.VMEM((2, n_pages * page, SB_WIDTH), F32),
            pltpu.VMEM((2, n_pages * page, SB_WIDTH), F32),
            pltpu.SemaphoreType.DMA((2, 2)),
        ],
    )
    out = pl.pallas_call(
        functools.partial(_sb_decode_kernel, n_pages=n_pages, page=page),
        grid_spec=grid_spec,
        out_shape=jax.ShapeDtypeStruct((rows, 1, SB_WIDTH), F32),
        compiler_params=_params("arbitrary"),
        name="sb_decode",
    )(page_table, sb_bias, q.reshape(rows, 1, SB_WIDTH), cache_k, cache_v)
    return out.reshape(rows, SB_WIDTH)


def _even_sample_post_kernel(x_ref, m_ref, gpost_ref, osb_ref, opool_ref, wout_ref, xo_ref):
    y = _dot(osb_ref[...].astype(BF16), wout_ref[0:SB_WIDTH, :]) + _dot(opool_ref[...], wout_ref[SB_WIDTH:, :])
    xo_ref[...] = x_ref[...] + m_ref[0, 2] * _rms(y, gpost_ref[...])


def _even_sample_post(x, mod, g_post, o_sb, o_pool, w_out):
    return pl.pallas_call(
        _even_sample_post_kernel,
        out_shape=jax.ShapeDtypeStruct(x.shape, F32),
        compiler_params=pltpu.CompilerParams(vmem_limit_bytes=VMEM_LIMIT_BYTES),
        name="even_sample_post",
    )(x, mod, g_post, o_sb, o_pool, w_out)


def _odd_sample_pre_kernel(x_ref, m_ref, gpre_ref, cos_ref, sin_ref, win_ref, q_ref, k_ref, v_ref, g_ref):
    hb = _adaln_pre(x_ref[...], m_ref, gpre_ref[...]).astype(BF16)
    cos, sin = cos_ref[...], sin_ref[...]
    for hd in range(RET_HEADS):
        qo = hd * RET_QK_DIM
        ko = RET_QK_WIDTH + hd * RET_QK_DIM
        q_ref[:, qo:qo + RET_QK_DIM] = _rotate(_dot(hb, win_ref[:, qo:qo + RET_QK_DIM]), cos, sin)
        k_ref[:, qo:qo + RET_QK_DIM] = _rotate(_dot(hb, win_ref[:, ko:ko + RET_QK_DIM]), cos, sin) * (RET_QK_DIM ** -0.5)
    vo = 2 * RET_QK_WIDTH
    v_ref[...] = _dot(hb, win_ref[:, vo:vo + RET_V_WIDTH])
    g_ref[...] = _dot(hb, win_ref[:, vo + RET_V_WIDTH:])


def _odd_sample_pre(x, mod, g_pre, cos, sin, w_in):
    rows = x.shape[0]
    outs = [
        jax.ShapeDtypeStruct((rows, RET_QK_WIDTH), F32),
        jax.ShapeDtypeStruct((rows, RET_QK_WIDTH), F32),
        jax.ShapeDtypeStruct((rows, RET_V_WIDTH), F32),
        jax.ShapeDtypeStruct((rows, RET_V_WIDTH), F32),
    ]
    return pl.pallas_call(
        _odd_sample_pre_kernel,
        out_shape=outs,
        compiler_params=pltpu.CompilerParams(vmem_limit_bytes=VMEM_LIMIT_BYTES),
        name="odd_sample_pre",
    )(x, mod, g_pre, cos, sin, w_in)


def _ret_step_kernel(gam_ref, q_ref, k_ref, v_ref, s_ref, o_ref, so_ref, *, nb):
    row0 = lax.broadcasted_iota(jnp.int32, (16, RET_QK_DIM), 0) == 0
    for b in range(nb):
        for hd in range(RET_HEADS):
            qh = q_ref[b, :, hd * RET_QK_DIM:(hd + 1) * RET_QK_DIM]
            kh = k_ref[b, :, hd * RET_QK_DIM:(hd + 1) * RET_QK_DIM]
            vh = v_ref[b, :, hd * RET_V_DIM:(hd + 1) * RET_V_DIM]
            qb = qh.astype(BF16)
            vb = vh.astype(BF16).astype(F32)
            score = jnp.sum(qb.astype(F32) * kh.astype(BF16).astype(F32), axis=-1, keepdims=True)
            state = s_ref[b, hd]
            q16 = jnp.where(row0, jnp.broadcast_to(qh, (16, RET_QK_DIM)), 0.0).astype(BF16)
            cross = _dot(q16, state.astype(BF16))[0:1] * gam_ref[hd]
            o_ref[b, :, hd * RET_V_DIM:(hd + 1) * RET_V_DIM] = score.astype(BF16).astype(F32) * vb + cross
            kcol = jnp.broadcast_to(kh, (8, RET_QK_DIM)).T[:, 0:1]
            so_ref[b, hd] = gam_ref[hd] * state + kcol * vh


def _ret_step(q, k, v, state, nb):
    rows = q.shape[0]
    gamma = jnp.exp(jnp.asarray(_ret_log_gamma()))
    grid_spec = pltpu.PrefetchScalarGridSpec(
        num_scalar_prefetch=1,
        grid=(rows // nb,),
        in_specs=[
            pl.BlockSpec((nb, 1, RET_QK_WIDTH), lambda i, *_: (i, 0, 0)),
            pl.BlockSpec((nb, 1, RET_QK_WIDTH), lambda i, *_: (i, 0, 0)),
            pl.BlockSpec((nb, 1, RET_V_WIDTH), lambda i, *_: (i, 0, 0)),
            pl.BlockSpec((nb, RET_HEADS, RET_QK_DIM, RET_V_DIM), lambda i, *_: (i, 0, 0, 0)),
        ],
        out_specs=[
            pl.BlockSpec((nb, 1, RET_V_WIDTH), lambda i, *_: (i, 0, 0)),
            pl.BlockSpec((nb, RET_HEADS, RET_QK_DIM, RET_V_DIM), lambda i, *_: (i, 0, 0, 0)),
        ],
    )
    o, s_new = pl.pallas_call(
        functools.partial(_ret_step_kernel, nb=nb),
        grid_spec=grid_spec,
        out_shape=[
            jax.ShapeDtypeStruct((rows, 1, RET_V_WIDTH), F32),
            jax.ShapeDtypeStruct(state.shape, F32),
        ],
        compiler_params=_params("arbitrary"),
        name="retention_step",
    )(gamma, q.reshape(rows, 1, RET_QK_WIDTH), k.reshape(rows, 1, RET_QK_WIDTH), v.reshape(rows, 1, RET_V_WIDTH), state)
    return o.reshape(rows, RET_V_WIDTH), s_new


def _odd_sample_post_kernel(x_ref, m_ref, gpost_ref, o_ref, g_ref, wout_ref, xo_ref, obuf):
    for hd in range(RET_HEADS):
        cols = slice(hd * RET_V_DIM, (hd + 1) * RET_V_DIM)
        obuf[:, cols] = _group_norm_gate(o_ref[:, cols], g_ref[:, cols]).astype(BF16)
    y = _dot(obuf[...], wout_ref[...])
    xo_ref[...] = x_ref[...] + m_ref[0, 2] * _rms(y, gpost_ref[...])


def _odd_sample_post(x, mod, g_post, o, g, w_out):
    return pl.pallas_call(
        _odd_sample_post_kernel,
        out_shape=jax.ShapeDtypeStruct(x.shape, F32),
        scratch_shapes=[pltpu.VMEM((x.shape[0], RET_V_WIDTH), BF16)],
        compiler_params=pltpu.CompilerParams(vmem_limit_bytes=VMEM_LIMIT_BYTES),
        name="odd_sample_post",
    )(x, mod, g_post, o, g, w_out)


def _sample_trunk(x, mods, cache_k, cache_v, state_pool, state_ret, page_table, norm_pre, norm_post, ffn_w_in,
                  ffn_w_out, w_in_even, sb_bias, pool_w, pool_scale, w_out_even, w_in_odd, w_out_odd):
    depth = mods.shape[0]
    rows = x.shape[0]
    n_pages = page_table.shape[1]
    page = cache_k.shape[2]
    x = x.reshape(rows, D_MODEL)
    ks, vs, pools, rets = [], [], [], []
    for l in range(depth):
        li = l // 2
        mod = lambda s: mods[l, 3 * s:3 * s + 3].reshape(1, 3, rows, D_MODEL)
        gp = lambda s: norm_pre[l, s].reshape(1, D_MODEL)
        gq = lambda s: norm_post[l, s].reshape(1, D_MODEL)
        x = _ffn_sublayer(x[None], mod(0), gp(0), gq(0), ffn_w_in[l, 0], ffn_w_out[l, 0], tm=rows)[0]
        if l % 2 == 0:
            pool_flat = state_pool[li].reshape(rows, (POOL_HIST - 1) * POOL_WIDTH)
            q, k, v, pool_new, o_pool = _even_sample_pre(x, mod(1), gp(1), w_in_even[li], pool_flat, pool_w[li],
                                                         pool_scale[li].reshape(1, POOL_WIDTH))
            o_sb = _sb_decode(q, sb_bias[li], page_table, cache_k[li].reshape(-1, page, SB_WIDTH),
                              cache_v[li].reshape(-1, page, SB_WIDTH))
            x = _even_sample_post(x, mod(1), gq(1), o_sb, o_pool, w_out_even[li])
            ks.append(k.reshape(rows, 1, SB_HEADS, SB_HEAD_DIM))
            vs.append(v.reshape(rows, 1, SB_HEADS, SB_HEAD_DIM))
            pools.append(pool_new.reshape(rows, POOL_HIST - 1, POOL_WIDTH))
        else:
            cos, sin = _rope_tables(jnp.full((1,), n_pages * page, F32))
            q, k, v, g = _odd_sample_pre(x, mod(1), gp(1), cos, sin, w_in_odd[li])
            o, s_new = _ret_step(q, k, v, state_ret[li], nb=2)
            x = _odd_sample_post(x, mod(1), gq(1), o, g, w_out_odd[li])
            rets.append(s_new)
        x = _ffn_sublayer(x[None], mod(2), gp(2), gq(2), ffn_w_in[l, 1], ffn_w_out[l, 1], tm=rows)[0]
    return x.reshape(rows, 1, D_MODEL), jnp.stack(ks, 0), jnp.stack(vs, 0), jnp.stack(pools, 0), jnp.stack(rets, 0)


def _prompt_trunk(x, modp, norm_pre, norm_post, ffn_w_in, ffn_w_out, w_in_even, sb_bias, pool_w, pool_scale,
                  w_out_even, w_in_odd, w_out_odd):
    depth = modp.shape[0]
    b = x.shape[0]
    ks, vs, pools, rets = [], [], [], []
    for l in range(depth):
        li = l // 2
        mod = lambda s: modp[l, 3 * s:3 * s + 3].transpose(1, 0, 2).reshape(b, 3, 1, D_MODEL)
        gp = lambda s: norm_pre[l, s].reshape(1, D_MODEL)
        gq = lambda s: norm_post[l, s].reshape(1, D_MODEL)
        x = _ffn_sublayer(x, mod(0), gp(0), gq(0), ffn_w_in[l, 0], ffn_w_out[l, 0], tm=512)
        if l % 2 == 0:
            x, k, v, tail = _even_prompt(x, mod(1), gp(1), gq(1), w_in_even[li], sb_bias[li], pool_w[li],
                                         pool_scale[li].reshape(1, POOL_WIDTH), w_out_even[li], tm=256, tq=256)
            ks.append(k.reshape(b, -1, SB_HEADS, SB_HEAD_DIM))
            vs.append(v.reshape(b, -1, SB_HEADS, SB_HEAD_DIM))
            pools.append(tail[:, 1:])
        else:
            x, s = _odd_prompt(x, mod(1), gp(1), gq(1), w_in_odd[li], w_out_odd[li], tm=256)
            rets.append(s)
        x = _ffn_sublayer(x, mod(2), gp(2), gq(2), ffn_w_in[l, 1], ffn_w_out[l, 1], tm=512)
    return x, jnp.stack(ks, 0), jnp.stack(vs, 0), jnp.stack(pools, 0), jnp.stack(rets, 0)


def kernel(x_prompt, x_sample, c_prompt, c_sample, cache_k, cache_v, state_pool, state_ret, page_table, w_ada, b_ada, norm_pre, norm_post, ffn_w_in, ffn_w_out, w_in_even, sb_bias, pool_w, pool_scale, w_out_even, w_in_odd, w_out_odd):
    modp, mods = _modulation(c_prompt, c_sample, w_ada, b_ada)
    weights = tuple(w.astype(BF16) for w in (ffn_w_in, ffn_w_out, w_in_even))
    weights += (sb_bias, pool_w.astype(BF16), pool_scale) + tuple(w.astype(BF16) for w in (w_out_even, w_in_odd, w_out_odd))
    y_p, kp, vp, pp, rp = _prompt_trunk(x_prompt, modp, norm_pre, norm_post, *weights)
    y_s, ks, vs, ps, rs = _sample_trunk(x_sample, mods, cache_k, cache_v, state_pool, state_ret, page_table,
                                        norm_pre, norm_post, *weights)
    return (y_p, y_s, kp, vp, pp, rp, ks, vs, ps, rs)
```

```python
import functools

import numpy as np
import jax
import jax.numpy as jnp
from jax import lax
from jax.experimental import pallas as pl
from jax.experimental.pallas import tpu as pltpu

F32 = jnp.float32
BF16 = jnp.bfloat16

D_MODEL = 1024
D_FF = 2816
N_SUB = 3
NORM_EPS = 1e-6
SB_HEADS = 8
SB_HEAD_DIM = 64
SB_WIDTH = SB_HEADS * SB_HEAD_DIM
N_PAIRS = SB_HEADS // 2
POOL_WINDOWS = (2, 4, 8, 16)
POOL_GROUP = 128
POOL_WIDTH = len(POOL_WINDOWS) * POOL_GROUP
POOL_HIST = 16
RET_HEADS = 4
RET_QK_DIM = 256
RET_V_DIM = 512
RET_QK_WIDTH = RET_HEADS * RET_QK_DIM
RET_V_WIDTH = RET_HEADS * RET_V_DIM
RET_CHUNK = 128
ROPE_BASE = 10000.0
LN_EPS = 1e-5
LANES = 128
KBLK = 256
SUFFIX_PAD = 16
LOG2E = 1.4426950408889634
VMEM_LIMIT_BYTES = 56 * 1024 * 1024


def _params(*sem):
    return pltpu.CompilerParams(dimension_semantics=sem, vmem_limit_bytes=VMEM_LIMIT_BYTES)


def _resident(shape):
    nd = len(shape)
    return pl.BlockSpec(shape, lambda *_: (0,) * nd, pipeline_mode=pl.Buffered(1))


def _rms(x, g):
    ms = jnp.mean(x * x, axis=-1, keepdims=True)
    return x * lax.rsqrt(ms + NORM_EPS) * g


def _adaln_pre(x, m_ref, g):
    return _rms(x, g) * (1.0 + m_ref[0, 1]) + m_ref[0, 0]


def _silu(x):
    return x * jax.nn.sigmoid(x)


def _dot(a, b):
    return jnp.dot(a, b, preferred_element_type=F32)


def _dot_nt(a, b):
    return lax.dot_general(a, b, (((1,), (1,)), ((), ())), preferred_element_type=F32)


def _dot_tn(a, b):
    return lax.dot_general(a, b, (((0,), (0,)), ((), ())), preferred_element_type=F32)


def _mod_kernel(cp_ref, cs_ref, w_ref, b_ref, mp_ref, ms_ref):
    w = w_ref[0].astype(BF16)
    b = b_ref[0, 0]
    mp_ref[0, 0] = _dot(_silu(cp_ref[...]).astype(BF16), w) + b
    ms_ref[0, 0] = _dot(_silu(cs_ref[...]).astype(BF16), w) + b


def _modulation(c_p, c_s, w_ada, b_ada):
    depth = w_ada.shape[0]
    nb = N_SUB * 3
    bp, bs = c_p.shape[0], c_s.shape[0]
    return pl.pallas_call(
        _mod_kernel,
        grid=(depth, nb),
        in_specs=[
            pl.BlockSpec((bp, D_MODEL), lambda l, n: (0, 0)),
            pl.BlockSpec((bs, D_MODEL), lambda l, n: (0, 0)),
            pl.BlockSpec((1, D_MODEL, D_MODEL), lambda l, n: (l, 0, n)),
            pl.BlockSpec((1, 1, 1, D_MODEL), lambda l, n: (l, n, 0, 0)),
        ],
        out_specs=[
            pl.BlockSpec((1, 1, bp, D_MODEL), lambda l, n: (l, n, 0, 0)),
            pl.BlockSpec((1, 1, bs, D_MODEL), lambda l, n: (l, n, 0, 0)),
        ],
        out_shape=[
            jax.ShapeDtypeStruct((depth, nb, bp, D_MODEL), F32),
            jax.ShapeDtypeStruct((depth, nb, bs, D_MODEL), F32),
        ],
        compiler_params=_params("arbitrary", "arbitrary"),
        name="adaln_modulation",
    )(c_p, c_s, w_ada, b_ada.reshape(depth, nb, 1, D_MODEL))


def _ffn_kernel(x_ref, m_ref, gpre_ref, gpost_ref, win_ref, wout_ref, o_ref):
    x = x_ref[0]
    hb = _adaln_pre(x, m_ref, gpre_ref[...]).astype(BF16)
    gu = _dot(hb, win_ref[...])
    a = (_silu(gu[:, :D_FF]) * gu[:, D_FF:]).astype(BF16)
    y = _dot(a, wout_ref[...])
    o_ref[0] = x + 0.5 * m_ref[0, 2] * _rms(y, gpost_ref[...])


def _ffn_sublayer(x, mod, g_pre, g_post, w_in, w_out, tm):
    g, t, _ = x.shape
    r = mod.shape[2]
    rb = 1 if r == 1 else tm
    return pl.pallas_call(
        _ffn_kernel,
        grid=(g, t // tm),
        in_specs=[
            pl.BlockSpec((1, tm, D_MODEL), lambda b, j: (b, j, 0)),
            pl.BlockSpec((1, 3, rb, D_MODEL), (lambda b, j: (b, 0, 0, 0)) if r == 1 else (lambda b, j: (b, 0, j, 0))),
            _resident((1, D_MODEL)),
            _resident((1, D_MODEL)),
            _resident((D_MODEL, 2 * D_FF)),
            _resident((D_FF, D_MODEL)),
        ],
        out_specs=pl.BlockSpec((1, tm, D_MODEL), lambda b, j: (b, j, 0)),
        out_shape=jax.ShapeDtypeStruct(x.shape, F32),
        compiler_params=_params("arbitrary", "arbitrary"),
        name="ffn_sublayer",
    )(x, mod, g_pre, g_post, w_in, w_out)


def _neg_abs(z):
    sign = jnp.uint32(0x80000000)
    return pltpu.bitcast(pltpu.bitcast(z, jnp.uint32) | sign, F32)


def _softplus(z):
    return jnp.maximum(z, 0.0) + jnp.log(1.0 + jnp.exp(-jnp.abs(z)))


def _even_prompt_kernel(bias_ref, x_ref, m_ref, gpre_ref, gpost_ref, win_ref, pw_ref, ps_ref, wout_ref,
                        xo_ref, k_ref, v_ref, tail_ref,
                        kbuf, vtbuf, ubuf, obuf, acc_ref, car_ref, *, tm, tq):
    j = pl.program_id(1)
    nj = pl.num_programs(1)
    x = x_ref[0]
    hb = _adaln_pre(x, m_ref, gpre_ref[...]).astype(BF16)
    qkvu = _dot(hb, win_ref[...])
    q = qkvu[:, 0:SB_WIDTH] * (SB_HEAD_DIM ** -0.5)
    k = qkvu[:, SB_WIDTH:2 * SB_WIDTH]
    v = qkvu[:, 2 * SB_WIDTH:3 * SB_WIDTH]
    u = qkvu[:, 3 * SB_WIDTH:]
    k_ref[0] = k
    v_ref[0] = v
    nsub = tm // KBLK
    for s in range(nsub):
        kbuf[j * nsub + s] = k[s * KBLK:(s + 1) * KBLK].astype(BF16)
        vtbuf[j * nsub + s] = v[s * KBLK:(s + 1) * KBLK].T.astype(BF16)

    lane = lax.broadcasted_iota(jnp.int32, (tq, 2 * SB_HEAD_DIM), 1)
    krow = lax.broadcasted_iota(jnp.int32, (KBLK, 2 * tq), 0)
    qcol = lax.broadcasted_iota(jnp.int32, (KBLK, 2 * tq), 1)
    qcol = jnp.where(qcol >= tq, qcol - tq, qcol)
    first_head = lax.broadcasted_iota(jnp.int32, (1, 2 * tq), 1) < tq
    mr = lax.broadcasted_iota(jnp.int32, (KBLK + SUFFIX_PAD, 2 * KBLK), 0)
    mc = lax.broadcasted_iota(jnp.int32, (KBLK + SUFFIX_PAD, 2 * KBLK), 1)
    mc = jnp.where(mc >= KBLK, mc - KBLK, mc)
    suffix_neg = jnp.where((mc > mr) | (mr == KBLK), -1.0, 0.0).astype(BF16)

    for qt in range(tm // tq):
        qtile = j * (tm // tq) + qt
        qrows = q[qt * tq:(qt + 1) * tq]
        qs_t, bias_row = [], []
        for p in range(N_PAIRS):
            q2 = qrows[:, 128 * p:128 * (p + 1)]
            qa = jnp.where(lane < SB_HEAD_DIM, q2, 0.0).T
            qb = jnp.where(lane >= SB_HEAD_DIM, q2, 0.0).T
            qs_t.append(jnp.concatenate([qa, qb], axis=1).astype(BF16))
            bias_row.append(jnp.where(first_head, bias_ref[2 * p], bias_ref[2 * p + 1]) * LOG2E)
        acc_ref[...] = jnp.zeros_like(acc_ref)
        car_ref[...] = jnp.zeros_like(car_ref)

        def step(kb, masked):
            kblk = kbuf[kb]
            vtb = vtbuf[kb]
            if masked:
                valid = (kb * KBLK + krow) < (qtile * tq + qcol)
            zs = [_dot(kblk[:, 128 * p:128 * (p + 1)], qs_t[p]) for p in range(N_PAIRS)]
            logbs, hls = [], []
            for p in range(N_PAIRS):
                z = zs[p] * LOG2E + bias_row[p]
                sp = jnp.maximum(z, 0.0) + jnp.log2(1.0 + jnp.exp2(_neg_abs(z)))
                logbs.append(z - sp)
                if masked:
                    sp = jnp.where(valid, sp, 0.0)
                hi = sp.astype(BF16)
                lo = (sp - hi.astype(F32)).astype(BF16)
                hls.append(jnp.concatenate([hi, lo], axis=0))
            sums = [_dot(suffix_neg, hls[p]) for p in range(N_PAIRS)]
            ws = []
            for p in range(N_PAIRS):
                w = jnp.exp2(logbs[p] + sums[p][0:KBLK] + car_ref[p])
                if masked:
                    w = jnp.where(valid, w, 0.0)
                ws.append(w.astype(BF16))
                car_ref[p] += sums[p][KBLK:KBLK + 1]
            for p in range(N_PAIRS):
                acc_ref[p] += _dot(vtb[128 * p:128 * (p + 1), :], ws[p])

        ndiag = tq // KBLK
        n_full = qtile * ndiag
        for dk in range(ndiag - 1, -1, -1):
            step(n_full + dk, True)

        def body(it, carry):
            step(n_full - 1 - it, False)
            return carry

        lax.fori_loop(0, n_full, body, 0)

        for p in range(N_PAIRS):
            a = acc_ref[p]
            o_t = jnp.concatenate([a[0:SB_HEAD_DIM, 0:tq], a[SB_HEAD_DIM:, tq:]], axis=0)
            obuf[qt * tq:(qt + 1) * tq, 128 * p:128 * (p + 1)] = o_t.T.astype(BF16)

    @pl.when(j == 0)
    def _():
        ubuf[0:POOL_HIST] = jnp.zeros((POOL_HIST, POOL_WIDTH), F32)

    ubuf[POOL_HIST:POOL_HIST + tm] = u
    pos1 = (j * tm + 1 + lax.broadcasted_iota(jnp.int32, (tm, POOL_GROUP), 0)).astype(F32)
    for gi, wdw in enumerate(POOL_WINDOWS):
        lo_, hi_ = gi * POOL_GROUP, (gi + 1) * POOL_GROUP
        ug = u[:, lo_:hi_]
        ssum = ug
        for sft in range(1, wdw):
            ssum = ssum + ubuf[POOL_HIST - sft:POOL_HIST - sft + tm, lo_:hi_]
        d = ssum / jnp.minimum(pos1, float(wdw)) - ug
        yg = _dot(d.astype(BF16), pw_ref[gi]) * ps_ref[:, lo_:hi_]
        obuf[:, SB_WIDTH + lo_:SB_WIDTH + hi_] = yg.astype(BF16)
    hist = ubuf[tm:tm + POOL_HIST]
    ubuf[0:POOL_HIST] = hist

    @pl.when(j == nj - 1)
    def _():
        tail_ref[0] = hist

    y = _dot(obuf[...], wout_ref[...])
    xo_ref[0] = x + m_ref[0, 2] * _rms(y, gpost_ref[...])


def _even_prompt(x, mod, g_pre, g_post, w_in, sb_bias, pool_w, pool_scale, w_out, tm, tq):
    b, t, _ = x.shape
    nkb = t // KBLK
    grid_spec = pltpu.PrefetchScalarGridSpec(
        num_scalar_prefetch=1,
        grid=(b, t // tm),
        in_specs=[
            pl.BlockSpec((1, tm, D_MODEL), lambda i, j, *_: (i, j, 0)),
            pl.BlockSpec((1, 3, 1, D_MODEL), lambda i, j, *_: (i, 0, 0, 0)),
            _resident((1, D_MODEL)),
            _resident((1, D_MODEL)),
            _resident(w_in.shape),
            _resident(pool_w.shape),
            _resident((1, POOL_WIDTH)),
            _resident(w_out.shape),
        ],
        out_specs=[
            pl.BlockSpec((1, tm, D_MODEL), lambda i, j, *_: (i, j, 0)),
            pl.BlockSpec((1, tm, SB_WIDTH), lambda i, j, *_: (i, j, 0)),
            pl.BlockSpec((1, tm, SB_WIDTH), lambda i, j, *_: (i, j, 0)),
            pl.BlockSpec((1, POOL_HIST, POOL_WIDTH), lambda i, j, *_: (i, 0, 0)),
        ],
        scratch_shapes=[
            pltpu.VMEM((nkb, KBLK, SB_WIDTH), BF16),
            pltpu.VMEM((nkb, SB_WIDTH, KBLK), BF16),
            pltpu.VMEM((POOL_HIST + tm, POOL_WIDTH), F32),
            pltpu.VMEM((tm, SB_WIDTH + POOL_WIDTH), BF16),
            pltpu.VMEM((N_PAIRS, 2 * SB_HEAD_DIM, 2 * tq), F32),
            pltpu.VMEM((N_PAIRS, 1, 2 * tq), F32),
        ],
    )
    return pl.pallas_call(
        functools.partial(_even_prompt_kernel, tm=tm, tq=tq),
        grid_spec=grid_spec,
        out_shape=[
            jax.ShapeDtypeStruct(x.shape, F32),
            jax.ShapeDtypeStruct((b, t, SB_WIDTH), F32),
            jax.ShapeDtypeStruct((b, t, SB_WIDTH), F32),
            jax.ShapeDtypeStruct((b, POOL_HIST, POOL_WIDTH), F32),
        ],
        compiler_params=_params("arbitrary", "arbitrary"),
        name="even_prompt",
    )(sb_bias, x, mod, g_pre, g_post, w_in, pool_w, pool_scale, w_out)


def _ret_log_gamma():
    return np.log(np.float32(1.0) - np.exp2(np.float32(-5.0) - np.arange(RET_HEADS, dtype=np.float32))).astype(np.float32)


def _ret_tables(chunk):
    lg = jnp.asarray(_ret_log_gamma())
    i = jnp.arange(chunk, dtype=F32)
    diff = i[:, None] - i[None, :]
    decay = jnp.where(diff[None] >= 0, jnp.exp(jnp.maximum(diff, 0.0)[None] * lg[:, None, None]), 0.0)
    cross = jnp.exp((i[None, :] + 1.0) * lg[:, None])[:, :, None]
    kdec = jnp.exp((chunk - 1.0 - i)[None, :] * lg[:, None])[:, :, None]
    full = jnp.exp(chunk * lg)
    return decay, cross, kdec, full


def _rope_tables(pos):
    half = RET_QK_DIM // 2
    inv = ROPE_BASE ** (-jnp.arange(half, dtype=F32) / half)
    ang = pos[:, None] * inv[None, :]
    return jnp.cos(ang), jnp.sin(ang)


def _rotate(x, cos, sin):
    half = RET_QK_DIM // 2
    x1, x2 = x[:, :half], x[:, half:]
    return jnp.concatenate([x1 * cos - x2 * sin, x1 * sin + x2 * cos], axis=-1)


def _group_norm_gate(o, g):
    mu = jnp.mean(o, axis=-1, keepdims=True)
    c = o - mu
    var = jnp.mean(c * c, axis=-1, keepdims=True)
    return c * lax.rsqrt(var + LN_EPS) * _silu(g)


def _odd_prompt_kernel(full_ref, x_ref, m_ref, gpre_ref, gpost_ref, cos_ref, sin_ref, win_ref, wout_ref,
                       dec_ref, crs_ref, kdc_ref, xo_ref, s_ref, obuf, *, tm):
    j = pl.program_id(1)

    @pl.when(j == 0)
    def _():
        s_ref[...] = jnp.zeros_like(s_ref)

    x = x_ref[0]
    hb = _adaln_pre(x, m_ref, gpre_ref[...]).astype(BF16)
    cos, sin = cos_ref[...], sin_ref[...]
    for hd in range(RET_HEADS):
        qo = hd * RET_QK_DIM
        ko = RET_QK_WIDTH + hd * RET_QK_DIM
        vo = 2 * RET_QK_WIDTH + hd * RET_V_DIM
        go = 2 * RET_QK_WIDTH + RET_V_WIDTH + hd * RET_V_DIM
        qr = _rotate(_dot(hb, win_ref[:, qo:qo + RET_QK_DIM]), cos, sin)
        kr = _rotate(_dot(hb, win_ref[:, ko:ko + RET_QK_DIM]), cos, sin) * (RET_QK_DIM ** -0.5)
        vh = _dot(hb, win_ref[:, vo:vo + RET_V_DIM]).astype(BF16)
        gh = _dot(hb, win_ref[:, go:go + RET_V_DIM])
        for c in range(tm // RET_CHUNK):
            rows = slice(c * RET_CHUNK, (c + 1) * RET_CHUNK)
            qc = qr[rows].astype(BF16)
            kc = kr[rows]
            vc = vh[rows]
            scores = _dot_nt(qc, kc.astype(BF16)) * dec_ref[hd]
            state = s_ref[0, hd]
            o = _dot(scores.astype(BF16), vc) + _dot(qc, state.astype(BF16)) * crs_ref[hd]
            kd = (kc * kdc_ref[hd]).astype(BF16)
            s_ref[0, hd] = full_ref[hd] * state + _dot_tn(kd, vc)
            obuf[rows, hd * RET_V_DIM:(hd + 1) * RET_V_DIM] = _group_norm_gate(o, gh[rows]).astype(BF16)
    y = _dot(obuf[...], wout_ref[...])
    xo_ref[0] = x + m_ref[0, 2] * _rms(y, gpost_ref[...])


def _odd_prompt(x, mod, g_pre, g_post, w_in, w_out, tm):
    b, t, _ = x.shape
    cos, sin = _rope_tables(jnp.arange(t, dtype=F32))
    decay, cross, kdec, full = _ret_tables(RET_CHUNK)
    half = RET_QK_DIM // 2
    grid_spec = pltpu.PrefetchScalarGridSpec(
        num_scalar_prefetch=1,
        grid=(b, t // tm),
        in_specs=[
            pl.BlockSpec((1, tm, D_MODEL), lambda i, j, *_: (i, j, 0)),
            pl.BlockSpec((1, 3, 1, D_MODEL), lambda i, j, *_: (i, 0, 0, 0)),
            _resident((1, D_MODEL)),
            _resident((1, D_MODEL)),
            pl.BlockSpec((tm, half), lambda i, j, *_: (j, 0)),
            pl.BlockSpec((tm, half), lambda i, j, *_: (j, 0)),
            _resident(w_in.shape),
            _resident(w_out.shape),
            _resident(decay.shape),
            _resident(cross.shape),
            _resident(kdec.shape),
        ],
        out_specs=[
            pl.BlockSpec((1, tm, D_MODEL), lambda i, j, *_: (i, j, 0)),
            pl.BlockSpec((1, RET_HEADS, RET_QK_DIM, RET_V_DIM), lambda i, j, *_: (i, 0, 0, 0)),
        ],
        scratch_shapes=[pltpu.VMEM((tm, RET_V_WIDTH), BF16)],
    )
    return pl.pallas_call(
        functools.partial(_odd_prompt_kernel, tm=tm),
        grid_spec=grid_spec,
        out_shape=[
            jax.ShapeDtypeStruct(x.shape, F32),
            jax.ShapeDtypeStruct((b, RET_HEADS, RET_QK_DIM, RET_V_DIM), F32),
        ],
        compiler_params=_params("arbitrary", "arbitrary"),
        name="odd_prompt",
    )(full, x, mod, g_pre, g_post, cos, sin, w_in, w_out, decay, cross, kdec)


def _even_sample_pre_kernel(x_ref, m_ref, gpre_ref, win_ref, pool_ref, pw_ref, ps_ref,
                            q_ref, k_ref, v_ref, pool_out_ref, opool_ref):
    hb = _adaln_pre(x_ref[...], m_ref, gpre_ref[...]).astype(BF16)
    qkvu = _dot(hb, win_ref[...])
    q_ref[...] = qkvu[:, 0:SB_WIDTH] * (SB_HEAD_DIM ** -0.5)
    k_ref[...] = qkvu[:, SB_WIDTH:2 * SB_WIDTH]
    v_ref[...] = qkvu[:, 2 * SB_WIDTH:3 * SB_WIDTH]
    u = qkvu[:, 3 * SB_WIDTH:]
    nh = POOL_HIST - 1
    for gi, wdw in enumerate(POOL_WINDOWS):
        lo_, hi_ = gi * POOL_GROUP, (gi + 1) * POOL_GROUP
        ug = u[:, lo_:hi_]
        ssum = ug
        for back in range(1, wdw):
            r = nh - back
            ssum = ssum + pool_ref[:, r * POOL_WIDTH + lo_:r * POOL_WIDTH + hi_]
        d = ssum / float(wdw) - ug
        opool_ref[:, lo_:hi_] = (_dot(d.astype(BF16), pw_ref[gi]) * ps_ref[:, lo_:hi_]).astype(BF16)
    pool_out_ref[:, 0:(nh - 1) * POOL_WIDTH] = pool_ref[:, POOL_WIDTH:nh * POOL_WIDTH]
    pool_out_ref[:, (nh - 1) * POOL_WIDTH:] = u


def _even_sample_pre(x, mod, g_pre, w_in, pool_flat, pool_w, pool_scale):
    rows = x.shape[0]
    outs = [
        jax.ShapeDtypeStruct((rows, SB_WIDTH), F32),
        jax.ShapeDtypeStruct((rows, SB_WIDTH), F32),
        jax.ShapeDtypeStruct((rows, SB_WIDTH), F32),
        jax.ShapeDtypeStruct(pool_flat.shape, F32),
        jax.ShapeDtypeStruct((rows, POOL_WIDTH), BF16),
    ]
    return pl.pallas_call(
        _even_sample_pre_kernel,
        out_shape=outs,
        compiler_params=pltpu.CompilerParams(vmem_limit_bytes=VMEM_LIMIT_BYTES),
        name="even_sample_pre",
    )(x, mod, g_pre, w_in, pool_flat, pool_w, pool_scale)


def _sb_decode_kernel(pt_ref, bias_ref, q_ref, ck_ref, cv_ref, o_ref, kbuf, vbuf, sem, *, n_pages, page):
    i = pl.program_id(0)
    n = pl.num_programs(0)
    past = n_pages * page

    prow = page * SB_HEADS

    def copies(sample, slot):
        out = []
        for p in range(n_pages):
            pg = pt_ref[sample, p]
            out.append(pltpu.make_async_copy(ck_ref.at[pg], kbuf.at[slot, pl.ds(p * prow, prow)], sem.at[0, slot]))
            out.append(pltpu.make_async_copy(cv_ref.at[pg], vbuf.at[slot, pl.ds(p * prow, prow)], sem.at[1, slot]))
        return out

    def head_rows(buf, slot, h):
        return buf[slot, pl.ds(h, past, stride=SB_HEADS), :].astype(BF16)

    @pl.when(i == 0)
    def _():
        for c in copies(0, 0):
            c.start()

    @pl.when(i + 1 < n)
    def _():
        for c in copies(i + 1, (i + 1) % 2):
            c.start()

    slot = i % 2
    for c in copies(i, slot):
        c.wait()

    q8 = q_ref[0]
    hrow = lax.broadcasted_iota(jnp.int32, (SB_HEADS, SB_HEAD_DIM), 0)
    bias = jnp.zeros((SB_HEADS, 1), F32)
    z = jnp.zeros((SB_HEADS, past), F32)
    for h in range(SB_HEADS):
        bias = jnp.where(hrow[:, 0:1] == h, bias_ref[h], bias)
        qh = jnp.where(hrow == h, q8, 0.0).astype(BF16)
        z = z + _dot_nt(qh, head_rows(kbuf, slot, h))
    z = z + bias
    sp = _softplus(z)
    logb = z - sp
    nblk = past // LANES
    st = jnp.concatenate([sp[:, c * LANES:(c + 1) * LANES] for c in range(nblk)], axis=0)
    hi = st.astype(BF16)
    lo = (st - hi.astype(F32)).astype(BF16)
    hl = jnp.concatenate([hi, lo], axis=1)
    jr = lax.broadcasted_iota(jnp.int32, (2 * LANES, LANES), 0)
    jr = jnp.where(jr >= LANES, jr - LANES, jr)
    sc = lax.broadcasted_iota(jnp.int32, (2 * LANES, LANES), 1)
    within = _dot(hl, jnp.where(jr > sc, -1.0, 0.0).astype(BF16))
    total = _dot(hl, jnp.full((2 * LANES, LANES), -1.0, BF16))
    carry = jnp.zeros((SB_HEADS, LANES), F32)
    excl = [None] * nblk
    for c in range(nblk - 1, -1, -1):
        excl[c] = within[c * SB_HEADS:(c + 1) * SB_HEADS] + carry
        carry = carry + total[c * SB_HEADS:(c + 1) * SB_HEADS]
    w = jnp.exp(logb + jnp.concatenate(excl, axis=1)).astype(BF16)
    o = jnp.zeros((SB_HEADS, SB_HEAD_DIM), F32)
    for h in range(SB_HEADS):
        oh = _dot(w, head_rows(vbuf, slot, h))
        o = jnp.where(hrow == h, oh, o)
    o_ref[0] = o


def _sb_decode(q, sb_bias, page_table, cache_k, cache_v):
    rows, n_pages = page_table.shape
    n_phys, page = cache_k.shape[0], cache_k.shape[1]
    cache_k = cache_k.reshape(n_phys, page * SB_HEADS, SB_HEAD_DIM)
    cache_v = cache_v.reshape(n_phys, page * SB_HEADS, SB_HEAD_DIM)
    head_block = (1, SB_HEADS, SB_HEAD_DIM)
    grid_spec = pltpu.PrefetchScalarGridSpec(
        num_scalar_prefetch=2,
        grid=(rows,),
        in_specs=[
            pl.BlockSpec(head_block, lambda i, *_: (i, 0, 0)),
            pl.BlockSpec(memory_space=pl.ANY),
            pl.BlockSpec(memory_space=pl.ANY),
        ],
        out_specs=pl.BlockSpec(head_block, lambda i, *_: (i, 0, 0)),
        scratch_shapes=[
            pltpu.VMEM((2, n_pages * page * SB_HEADS, SB_HEAD_DIM), F32),
            pltpu.VMEM((2, n_pages * page * SB_HEADS, SB_HEAD_DIM), F32),
            pltpu.SemaphoreType.DMA((2, 2)),
        ],
    )
    out = pl.pallas_call(
        functools.partial(_sb_decode_kernel, n_pages=n_pages, page=page),
        grid_spec=grid_spec,
        out_shape=jax.ShapeDtypeStruct((rows, SB_HEADS, SB_HEAD_DIM), F32),
        compiler_params=_params("arbitrary"),
        name="sb_decode",
    )(page_table, sb_bias, q.reshape(rows, SB_HEADS, SB_HEAD_DIM), cache_k, cache_v)
    return out.reshape(rows, SB_WIDTH)


def _even_sample_post_kernel(x_ref, m_ref, gpost_ref, osb_ref, opool_ref, wout_ref, xo_ref):
    y = _dot(osb_ref[...].astype(BF16), wout_ref[0:SB_WIDTH, :]) + _dot(opool_ref[...], wout_ref[SB_WIDTH:, :])
    xo_ref[...] = x_ref[...] + m_ref[0, 2] * _rms(y, gpost_ref[...])


def _even_sample_post(x, mod, g_post, o_sb, o_pool, w_out):
    return pl.pallas_call(
        _even_sample_post_kernel,
        out_shape=jax.ShapeDtypeStruct(x.shape, F32),
        compiler_params=pltpu.CompilerParams(vmem_limit_bytes=VMEM_LIMIT_BYTES),
        name="even_sample_post",
    )(x, mod, g_post, o_sb, o_pool, w_out)


def _odd_sample_pre_kernel(x_ref, m_ref, gpre_ref, cos_ref, sin_ref, win_ref, q_ref, k_ref, v_ref, g_ref):
    hb = _adaln_pre(x_ref[...], m_ref, gpre_ref[...]).astype(BF16)
    cos, sin = cos_ref[...], sin_ref[...]
    for hd in range(RET_HEADS):
        qo = hd * RET_QK_DIM
        ko = RET_QK_WIDTH + hd * RET_QK_DIM
        q_ref[:, qo:qo + RET_QK_DIM] = _rotate(_dot(hb, win_ref[:, qo:qo + RET_QK_DIM]), cos, sin)
        k_ref[:, qo:qo + RET_QK_DIM] = _rotate(_dot(hb, win_ref[:, ko:ko + RET_QK_DIM]), cos, sin) * (RET_QK_DIM ** -0.5)
    vo = 2 * RET_QK_WIDTH
    v_ref[...] = _dot(hb, win_ref[:, vo:vo + RET_V_WIDTH])
    g_ref[...] = _dot(hb, win_ref[:, vo + RET_V_WIDTH:])


def _odd_sample_pre(x, mod, g_pre, cos, sin, w_in):
    rows = x.shape[0]
    outs = [
        jax.ShapeDtypeStruct((rows, RET_QK_WIDTH), F32),
        jax.ShapeDtypeStruct((rows, RET_QK_WIDTH), F32),
        jax.ShapeDtypeStruct((rows, RET_V_WIDTH), F32),
        jax.ShapeDtypeStruct((rows, RET_V_WIDTH), F32),
    ]
    return pl.pallas_call(
        _odd_sample_pre_kernel,
        out_shape=outs,
        compiler_params=pltpu.CompilerParams(vmem_limit_bytes=VMEM_LIMIT_BYTES),
        name="odd_sample_pre",
    )(x, mod, g_pre, cos, sin, w_in)


def _ret_step_kernel(gam_ref, q_ref, k_ref, v_ref, s_ref, o_ref, so_ref, *, nb):
    row0 = lax.broadcasted_iota(jnp.int32, (16, RET_QK_DIM), 0) == 0
    for b in range(nb):
        for hd in range(RET_HEADS):
            qh = q_ref[b, :, hd * RET_QK_DIM:(hd + 1) * RET_QK_DIM]
            kh = k_ref[b, :, hd * RET_QK_DIM:(hd + 1) * RET_QK_DIM]
            vh = v_ref[b, :, hd * RET_V_DIM:(hd + 1) * RET_V_DIM]
            qb = qh.astype(BF16)
            vb = vh.astype(BF16).astype(F32)
            score = jnp.sum(qb.astype(F32) * kh.astype(BF16).astype(F32), axis=-1, keepdims=True)
            state = s_ref[b, hd]
            q16 = jnp.where(row0, jnp.broadcast_to(qh, (16, RET_QK_DIM)), 0.0).astype(BF16)
            cross = _dot(q16, state.astype(BF16))[0:1] * gam_ref[hd]
            o_ref[b, :, hd * RET_V_DIM:(hd + 1) * RET_V_DIM] = score.astype(BF16).astype(F32) * vb + cross
            kcol = jnp.broadcast_to(kh, (8, RET_QK_DIM)).T[:, 0:1]
            so_ref[b, hd] = gam_ref[hd] * state + kcol * vh


def _ret_step(q, k, v, state, nb):
    rows = q.shape[0]
    gamma = jnp.exp(jnp.asarray(_ret_log_gamma()))
    grid_spec = pltpu.PrefetchScalarGridSpec(
        num_scalar_prefetch=1,
        grid=(rows // nb,),
        in_specs=[
            pl.BlockSpec((nb, 1, RET_QK_WIDTH), lambda i, *_: (i, 0, 0)),
            pl.BlockSpec((nb, 1, RET_QK_WIDTH), lambda i, *_: (i, 0, 0)),
            pl.BlockSpec((nb, 1, RET_V_WIDTH), lambda i, *_: (i, 0, 0)),
            pl.BlockSpec((nb, RET_HEADS, RET_QK_DIM, RET_V_DIM), lambda i, *_: (i, 0, 0, 0)),
        ],
        out_specs=[
            pl.BlockSpec((nb, 1, RET_V_WIDTH), lambda i, *_: (i, 0, 0)),
            pl.BlockSpec((nb, RET_HEADS, RET_QK_DIM, RET_V_DIM), lambda i, *_: (i, 0, 0, 0)),
        ],
    )
    o, s_new = pl.pallas_call(
        functools.partial(_ret_step_kernel, nb=nb),
        grid_spec=grid_spec,
        out_shape=[
            jax.ShapeDtypeStruct((rows, 1, RET_V_WIDTH), F32),
            jax.ShapeDtypeStruct(state.shape, F32),
        ],
        compiler_params=_params("arbitrary"),
        name="retention_step",
    )(gamma, q.reshape(rows, 1, RET_QK_WIDTH), k.reshape(rows, 1, RET_QK_WIDTH), v.reshape(rows, 1, RET_V_WIDTH), state)
    return o.reshape(rows, RET_V_WIDTH), s_new


def _odd_sample_post_kernel(x_ref, m_ref, gpost_ref, o_ref, g_ref, wout_ref, xo_ref, obuf):
    for hd in range(RET_HEADS):
        cols = slice(hd * RET_V_DIM, (hd + 1) * RET_V_DIM)
        obuf[:, cols] = _group_norm_gate(o_ref[:, cols], g_ref[:, cols]).astype(BF16)
    y = _dot(obuf[...], wout_ref[...])
    xo_ref[...] = x_ref[...] + m_ref[0, 2] * _rms(y, gpost_ref[...])


def _odd_sample_post(x, mod, g_post, o, g, w_out):
    return pl.pallas_call(
        _odd_sample_post_kernel,
        out_shape=jax.ShapeDtypeStruct(x.shape, F32),
        scratch_shapes=[pltpu.VMEM((x.shape[0], RET_V_WIDTH), BF16)],
        compiler_params=pltpu.CompilerParams(vmem_limit_bytes=VMEM_LIMIT_BYTES),
        name="odd_sample_post",
    )(x, mod, g_post, o, g, w_out)


def _sample_trunk(x, mods, cache_k, cache_v, state_pool, state_ret, page_table, norm_pre, norm_post, ffn_w_in,
                  ffn_w_out, w_in_even, sb_bias, pool_w, pool_scale, w_out_even, w_in_odd, w_out_odd):
    depth = mods.shape[0]
    rows = x.shape[0]
    n_pages = page_table.shape[1]
    page = cache_k.shape[2]
    x = x.reshape(rows, D_MODEL)
    ks, vs, pools, rets = [], [], [], []
    for l in range(depth):
        li = l // 2
        mod = lambda s: mods[l, 3 * s:3 * s + 3].reshape(1, 3, rows, D_MODEL)
        gp = lambda s: norm_pre[l, s].reshape(1, D_MODEL)
        gq = lambda s: norm_post[l, s].reshape(1, D_MODEL)
        x = _ffn_sublayer(x[None], mod(0), gp(0), gq(0), ffn_w_in[l, 0], ffn_w_out[l, 0], tm=rows)[0]
        if l % 2 == 0:
            pool_flat = state_pool[li].reshape(rows, (POOL_HIST - 1) * POOL_WIDTH)
            q, k, v, pool_new, o_pool = _even_sample_pre(x, mod(1), gp(1), w_in_even[li], pool_flat, pool_w[li],
                                                         pool_scale[li].reshape(1, POOL_WIDTH))
            o_sb = _sb_decode(q, sb_bias[li], page_table, cache_k[li], cache_v[li])
            x = _even_sample_post(x, mod(1), gq(1), o_sb, o_pool, w_out_even[li])
            ks.append(k.reshape(rows, 1, SB_HEADS, SB_HEAD_DIM))
            vs.append(v.reshape(rows, 1, SB_HEADS, SB_HEAD_DIM))
            pools.append(pool_new.reshape(rows, POOL_HIST - 1, POOL_WIDTH))
        else:
            cos, sin = _rope_tables(jnp.full((1,), n_pages * page, F32))
            q, k, v, g = _odd_sample_pre(x, mod(1), gp(1), cos, sin, w_in_odd[li])
            o, s_new = _ret_step(q, k, v, state_ret[li], nb=2)
            x = _odd_sample_post(x, mod(1), gq(1), o, g, w_out_odd[li])
            rets.append(s_new)
        x = _ffn_sublayer(x[None], mod(2), gp(2), gq(2), ffn_w_in[l, 1], ffn_w_out[l, 1], tm=rows)[0]
    return x.reshape(rows, 1, D_MODEL), jnp.stack(ks, 0), jnp.stack(vs, 0), jnp.stack(pools, 0), jnp.stack(rets, 0)


def _prompt_trunk(x, modp, norm_pre, norm_post, ffn_w_in, ffn_w_out, w_in_even, sb_bias, pool_w, pool_scale,
                  w_out_even, w_in_odd, w_out_odd):
    depth = modp.shape[0]
    b = x.shape[0]
    ks, vs, pools, rets = [], [], [], []
    for l in range(depth):
        li = l // 2
        mod = lambda s: modp[l, 3 * s:3 * s + 3].transpose(1, 0, 2).reshape(b, 3, 1, D_MODEL)
        gp = lambda s: norm_pre[l, s].reshape(1, D_MODEL)
        gq = lambda s: norm_post[l, s].reshape(1, D_MODEL)
        x = _ffn_sublayer(x, mod(0), gp(0), gq(0), ffn_w_in[l, 0], ffn_w_out[l, 0], tm=512)
        if l % 2 == 0:
            x, k, v, tail = _even_prompt(x, mod(1), gp(1), gq(1), w_in_even[li], sb_bias[li], pool_w[li],
                                         pool_scale[li].reshape(1, POOL_WIDTH), w_out_even[li], tm=256, tq=256)
            ks.append(k.reshape(b, -1, SB_HEADS, SB_HEAD_DIM))
            vs.append(v.reshape(b, -1, SB_HEADS, SB_HEAD_DIM))
            pools.append(tail[:, 1:])
        else:
            x, s = _odd_prompt(x, mod(1), gp(1), gq(1), w_in_odd[li], w_out_odd[li], tm=256)
            rets.append(s)
        x = _ffn_sublayer(x, mod(2), gp(2), gq(2), ffn_w_in[l, 1], ffn_w_out[l, 1], tm=512)
    return x, jnp.stack(ks, 0), jnp.stack(vs, 0), jnp.stack(pools, 0), jnp.stack(rets, 0)


def kernel(x_prompt, x_sample, c_prompt, c_sample, cache_k, cache_v, state_pool, state_ret, page_table, w_ada, b_ada, norm_pre, norm_post, ffn_w_in, ffn_w_out, w_in_even, sb_bias, pool_w, pool_scale, w_out_even, w_in_odd, w_out_odd):
    modp, mods = _modulation(c_prompt, c_sample, w_ada, b_ada)
    weights = tuple(w.astype(BF16) for w in (ffn_w_in, ffn_w_out, w_in_even))
    weights += (sb_bias, pool_w.astype(BF16), pool_scale) + tuple(w.astype(BF16) for w in (w_out_even, w_in_odd, w_out_odd))
    y_p, kp, vp, pp, rp = _prompt_trunk(x_prompt, modp, norm_pre, norm_post, *weights)
    y_s, ks, vs, ps, rs = _sample_trunk(x_sample, mods, cache_k, cache_v, state_pool, state_ret, page_table,
                                        norm_pre, norm_post, *weights)
    return (y_p, y_s, kp, vp, pp, rp, ks, vs, ps, rs)
```

```python
import functools

import numpy as np
import jax
import jax.numpy as jnp
from jax import lax
from jax.experimental import pallas as pl
from jax.experimental.pallas import tpu as pltpu

F32 = jnp.float32
BF16 = jnp.bfloat16

D_MODEL = 1024
D_FF = 2816
N_SUB = 3
NORM_EPS = 1e-6
SB_HEADS = 8
SB_HEAD_DIM = 64
SB_WIDTH = SB_HEADS * SB_HEAD_DIM
N_PAIRS = SB_HEADS // 2
POOL_WINDOWS = (2, 4, 8, 16)
POOL_GROUP = 128
POOL_WIDTH = len(POOL_WINDOWS) * POOL_GROUP
POOL_HIST = 16
RET_HEADS = 4
RET_QK_DIM = 256
RET_V_DIM = 512
RET_QK_WIDTH = RET_HEADS * RET_QK_DIM
RET_V_WIDTH = RET_HEADS * RET_V_DIM
RET_CHUNK = 128
ROPE_BASE = 10000.0
LN_EPS = 1e-5
LANES = 128
KBLK = 256
SUFFIX_PAD = 16
LOG2E = 1.4426950408889634
VMEM_LIMIT_BYTES = 56 * 1024 * 1024


def _params(*sem):
    return pltpu.CompilerParams(dimension_semantics=sem, vmem_limit_bytes=VMEM_LIMIT_BYTES)


def _resident(shape):
    nd = len(shape)
    return pl.BlockSpec(shape, lambda *_: (0,) * nd, pipeline_mode=pl.Buffered(1))


def _rms(x, g):
    ms = jnp.mean(x * x, axis=-1, keepdims=True)
    return x * lax.rsqrt(ms + NORM_EPS) * g


def _adaln_pre(x, m_ref, g):
    return _rms(x, g) * (1.0 + m_ref[0, 1]) + m_ref[0, 0]


def _silu(x):
    return x * jax.nn.sigmoid(x)


def _dot(a, b):
    return jnp.dot(a, b, preferred_element_type=F32)


def _dot_nt(a, b):
    return lax.dot_general(a, b, (((1,), (1,)), ((), ())), preferred_element_type=F32)


def _dot_tn(a, b):
    return lax.dot_general(a, b, (((0,), (0,)), ((), ())), preferred_element_type=F32)


def _mod_kernel(cp_ref, cs_ref, w_ref, b_ref, mp_ref, ms_ref):
    w = w_ref[0].astype(BF16)
    b = b_ref[0, 0]
    mp_ref[0, 0] = _dot(_silu(cp_ref[...]).astype(BF16), w) + b
    ms_ref[0, 0] = _dot(_silu(cs_ref[...]).astype(BF16), w) + b


def _modulation(c_p, c_s, w_ada, b_ada):
    depth = w_ada.shape[0]
    nb = N_SUB * 3
    bp, bs = c_p.shape[0], c_s.shape[0]
    return pl.pallas_call(
        _mod_kernel,
        grid=(depth, nb),
        in_specs=[
            pl.BlockSpec((bp, D_MODEL), lambda l, n: (0, 0)),
            pl.BlockSpec((bs, D_MODEL), lambda l, n: (0, 0)),
            pl.BlockSpec((1, D_MODEL, D_MODEL), lambda l, n: (l, 0, n)),
            pl.BlockSpec((1, 1, 1, D_MODEL), lambda l, n: (l, n, 0, 0)),
        ],
        out_specs=[
            pl.BlockSpec((1, 1, bp, D_MODEL), lambda l, n: (l, n, 0, 0)),
            pl.BlockSpec((1, 1, bs, D_MODEL), lambda l, n: (l, n, 0, 0)),
        ],
        out_shape=[
            jax.ShapeDtypeStruct((depth, nb, bp, D_MODEL), F32),
            jax.ShapeDtypeStruct((depth, nb, bs, D_MODEL), F32),
        ],
        compiler_params=_params("arbitrary", "arbitrary"),
        name="adaln_modulation",
    )(c_p, c_s, w_ada, b_ada.reshape(depth, nb, 1, D_MODEL))


def _ffn_kernel(x_ref, m_ref, gpre_ref, gpost_ref, win_ref, wout_ref, o_ref):
    x = x_ref[0]
    hb = _adaln_pre(x, m_ref, gpre_ref[...]).astype(BF16)
    gu = _dot(hb, win_ref[...])
    a = (_silu(gu[:, :D_FF]) * gu[:, D_FF:]).astype(BF16)
    y = _dot(a, wout_ref[...])
    o_ref[0] = x + 0.5 * m_ref[0, 2] * _rms(y, gpost_ref[...])


def _ffn_sublayer(x, mod, g_pre, g_post, w_in, w_out, tm):
    g, t, _ = x.shape
    r = mod.shape[2]
    rb = 1 if r == 1 else tm
    return pl.pallas_call(
        _ffn_kernel,
        grid=(g, t // tm),
        in_specs=[
            pl.BlockSpec((1, tm, D_MODEL), lambda b, j: (b, j, 0)),
            pl.BlockSpec((1, 3, rb, D_MODEL), (lambda b, j: (b, 0, 0, 0)) if r == 1 else (lambda b, j: (b, 0, j, 0))),
            _resident((1, D_MODEL)),
            _resident((1, D_MODEL)),
            _resident((D_MODEL, 2 * D_FF)),
            _resident((D_FF, D_MODEL)),
        ],
        out_specs=pl.BlockSpec((1, tm, D_MODEL), lambda b, j: (b, j, 0)),
        out_shape=jax.ShapeDtypeStruct(x.shape, F32),
        compiler_params=_params("arbitrary", "arbitrary"),
        name="ffn_sublayer",
    )(x, mod, g_pre, g_post, w_in, w_out)


def _softplus(z):
    return jnp.maximum(z, 0.0) + jnp.log(1.0 + jnp.exp(-jnp.abs(z)))


def _even_prompt_kernel(bias_ref, x_ref, m_ref, gpre_ref, gpost_ref, win_ref, pw_ref, ps_ref, wout_ref,
                        xo_ref, k_ref, v_ref, tail_ref,
                        kbuf, vtbuf, ubuf, obuf, acc_ref, car_ref, *, tm, tq):
    j = pl.program_id(1)
    nj = pl.num_programs(1)
    x = x_ref[0]
    hb = _adaln_pre(x, m_ref, gpre_ref[...]).astype(BF16)
    qkvu = _dot(hb, win_ref[...])
    q = qkvu[:, 0:SB_WIDTH] * (SB_HEAD_DIM ** -0.5)
    k = qkvu[:, SB_WIDTH:2 * SB_WIDTH]
    v = qkvu[:, 2 * SB_WIDTH:3 * SB_WIDTH]
    u = qkvu[:, 3 * SB_WIDTH:]
    v_t = v.T
    k_ref[0] = k.T
    v_ref[0] = v_t
    nsub = tm // KBLK
    for s in range(nsub):
        kbuf[j * nsub + s] = k[s * KBLK:(s + 1) * KBLK].astype(BF16)
        vtbuf[j * nsub + s] = v_t[:, s * KBLK:(s + 1) * KBLK].astype(BF16)

    lane = lax.broadcasted_iota(jnp.int32, (tq, 2 * SB_HEAD_DIM), 1)
    krow = lax.broadcasted_iota(jnp.int32, (KBLK, 2 * tq), 0)
    qcol = lax.broadcasted_iota(jnp.int32, (KBLK, 2 * tq), 1)
    qcol = jnp.where(qcol >= tq, qcol - tq, qcol)
    first_head = lax.broadcasted_iota(jnp.int32, (1, 2 * tq), 1) < tq
    mr = lax.broadcasted_iota(jnp.int32, (KBLK + SUFFIX_PAD, 2 * KBLK), 0)
    mc = lax.broadcasted_iota(jnp.int32, (KBLK + SUFFIX_PAD, 2 * KBLK), 1)
    mc = jnp.where(mc >= KBLK, mc - KBLK, mc)
    suffix_neg = jnp.where((mc > mr) | (mr == KBLK), -1.0, 0.0).astype(BF16)

    for qt in range(tm // tq):
        qtile = j * (tm // tq) + qt
        qrows = q[qt * tq:(qt + 1) * tq]
        qs_t, bias_row = [], []
        for p in range(N_PAIRS):
            q2 = qrows[:, 128 * p:128 * (p + 1)]
            qa = jnp.where(lane < SB_HEAD_DIM, q2, 0.0).T
            qb = jnp.where(lane >= SB_HEAD_DIM, q2, 0.0).T
            qs_t.append(jnp.concatenate([qa, qb], axis=1).astype(BF16))
            bias_row.append(jnp.where(first_head, bias_ref[2 * p], bias_ref[2 * p + 1]) * LOG2E)
        acc_ref[...] = jnp.zeros_like(acc_ref)
        car_ref[...] = jnp.zeros_like(car_ref)

        def step(kb, masked):
            kblk = kbuf[kb]
            vtb = vtbuf[kb]
            if masked:
                valid = (kb * KBLK + krow) < (qtile * tq + qcol)
            zs = [_dot(kblk[:, 128 * p:128 * (p + 1)], qs_t[p]) for p in range(N_PAIRS)]
            logbs, hls = [], []
            for p in range(N_PAIRS):
                z = zs[p] * LOG2E + bias_row[p]
                sp = jnp.maximum(z, 0.0) + jnp.log2(1.0 + jnp.exp2(-jnp.abs(z)))
                logbs.append(z - sp)
                if masked:
                    sp = jnp.where(valid, sp, 0.0)
                hi = sp.astype(BF16)
                lo = (sp - hi.astype(F32)).astype(BF16)
                hls.append(jnp.concatenate([hi, lo], axis=0))
            sums = [_dot(suffix_neg, hls[p]) for p in range(N_PAIRS)]
            ws = []
            for p in range(N_PAIRS):
                w = jnp.exp2(logbs[p] + sums[p][0:KBLK] + car_ref[p])
                if masked:
                    w = jnp.where(valid, w, 0.0)
                ws.append(w.astype(BF16))
                car_ref[p] += sums[p][KBLK:KBLK + 1]
            for p in range(N_PAIRS):
                acc_ref[p] += _dot(vtb[128 * p:128 * (p + 1), :], ws[p])

        ndiag = tq // KBLK
        n_full = qtile * ndiag
        for dk in range(ndiag - 1, -1, -1):
            step(n_full + dk, True)

        def body(it, carry):
            step(n_full - 1 - it, False)
            return carry

        lax.fori_loop(0, n_full, body, 0)

        for p in range(N_PAIRS):
            a = acc_ref[p]
            o_t = jnp.concatenate([a[0:SB_HEAD_DIM, 0:tq], a[SB_HEAD_DIM:, tq:]], axis=0)
            obuf[qt * tq:(qt + 1) * tq, 128 * p:128 * (p + 1)] = o_t.T.astype(BF16)

    @pl.when(j == 0)
    def _():
        ubuf[0:POOL_HIST] = jnp.zeros((POOL_HIST, POOL_WIDTH), F32)

    ubuf[POOL_HIST:POOL_HIST + tm] = u
    pos1 = (j * tm + 1 + lax.broadcasted_iota(jnp.int32, (tm, POOL_GROUP), 0)).astype(F32)
    for gi, wdw in enumerate(POOL_WINDOWS):
        lo_, hi_ = gi * POOL_GROUP, (gi + 1) * POOL_GROUP
        ug = u[:, lo_:hi_]
        ssum = ug
        for sft in range(1, wdw):
            ssum = ssum + ubuf[POOL_HIST - sft:POOL_HIST - sft + tm, lo_:hi_]
        d = ssum / jnp.minimum(pos1, float(wdw)) - ug
        yg = _dot(d.astype(BF16), pw_ref[gi]) * ps_ref[:, lo_:hi_]
        obuf[:, SB_WIDTH + lo_:SB_WIDTH + hi_] = yg.astype(BF16)
    hist = ubuf[tm:tm + POOL_HIST]
    ubuf[0:POOL_HIST] = hist

    @pl.when(j == nj - 1)
    def _():
        tail_ref[0] = hist

    y = _dot(obuf[...], wout_ref[...])
    xo_ref[0] = x + m_ref[0, 2] * _rms(y, gpost_ref[...])


def _even_prompt(x, mod, g_pre, g_post, w_in, sb_bias, pool_w, pool_scale, w_out, tm, tq):
    b, t, _ = x.shape
    nkb = t // KBLK
    grid_spec = pltpu.PrefetchScalarGridSpec(
        num_scalar_prefetch=1,
        grid=(b, t // tm),
        in_specs=[
            pl.BlockSpec((1, tm, D_MODEL), lambda i, j, *_: (i, j, 0)),
            pl.BlockSpec((1, 3, 1, D_MODEL), lambda i, j, *_: (i, 0, 0, 0)),
            _resident((1, D_MODEL)),
            _resident((1, D_MODEL)),
            _resident(w_in.shape),
            _resident(pool_w.shape),
            _resident((1, POOL_WIDTH)),
            _resident(w_out.shape),
        ],
        out_specs=[
            pl.BlockSpec((1, tm, D_MODEL), lambda i, j, *_: (i, j, 0)),
            pl.BlockSpec((1, SB_WIDTH, tm), lambda i, j, *_: (i, 0, j)),
            pl.BlockSpec((1, SB_WIDTH, tm), lambda i, j, *_: (i, 0, j)),
            pl.BlockSpec((1, POOL_HIST, POOL_WIDTH), lambda i, j, *_: (i, 0, 0)),
        ],
        scratch_shapes=[
            pltpu.VMEM((nkb, KBLK, SB_WIDTH), BF16),
            pltpu.VMEM((nkb, SB_WIDTH, KBLK), BF16),
            pltpu.VMEM((POOL_HIST + tm, POOL_WIDTH), F32),
            pltpu.VMEM((tm, SB_WIDTH + POOL_WIDTH), BF16),
            pltpu.VMEM((N_PAIRS, 2 * SB_HEAD_DIM, 2 * tq), F32),
            pltpu.VMEM((N_PAIRS, 1, 2 * tq), F32),
        ],
    )
    return pl.pallas_call(
        functools.partial(_even_prompt_kernel, tm=tm, tq=tq),
        grid_spec=grid_spec,
        out_shape=[
            jax.ShapeDtypeStruct(x.shape, F32),
            jax.ShapeDtypeStruct((b, SB_WIDTH, t), F32),
            jax.ShapeDtypeStruct((b, SB_WIDTH, t), F32),
            jax.ShapeDtypeStruct((b, POOL_HIST, POOL_WIDTH), F32),
        ],
        compiler_params=_params("arbitrary", "arbitrary"),
        name="even_prompt",
    )(sb_bias, x, mod, g_pre, g_post, w_in, pool_w, pool_scale, w_out)


def _ret_log_gamma():
    return np.log(np.float32(1.0) - np.exp2(np.float32(-5.0) - np.arange(RET_HEADS, dtype=np.float32))).astype(np.float32)


def _ret_tables(chunk):
    lg = jnp.asarray(_ret_log_gamma())
    i = jnp.arange(chunk, dtype=F32)
    diff = i[:, None] - i[None, :]
    decay = jnp.where(diff[None] >= 0, jnp.exp(jnp.maximum(diff, 0.0)[None] * lg[:, None, None]), 0.0)
    cross = jnp.exp((i[None, :] + 1.0) * lg[:, None])[:, :, None]
    kdec = jnp.exp((chunk - 1.0 - i)[None, :] * lg[:, None])[:, :, None]
    full = jnp.exp(chunk * lg)
    return decay, cross, kdec, full


def _rope_tables(pos):
    half = RET_QK_DIM // 2
    inv = ROPE_BASE ** (-jnp.arange(half, dtype=F32) / half)
    ang = pos[:, None] * inv[None, :]
    return jnp.cos(ang), jnp.sin(ang)


def _rotate(x, cos, sin):
    half = RET_QK_DIM // 2
    x1, x2 = x[:, :half], x[:, half:]
    return jnp.concatenate([x1 * cos - x2 * sin, x1 * sin + x2 * cos], axis=-1)


def _group_norm_gate(o, g):
    mu = jnp.mean(o, axis=-1, keepdims=True)
    c = o - mu
    var = jnp.mean(c * c, axis=-1, keepdims=True)
    return c * lax.rsqrt(var + LN_EPS) * _silu(g)


def _odd_prompt_kernel(full_ref, x_ref, m_ref, gpre_ref, gpost_ref, cos_ref, sin_ref, win_ref, wout_ref,
                       dec_ref, crs_ref, kdc_ref, xo_ref, s_ref, obuf, *, tm):
    j = pl.program_id(1)

    @pl.when(j == 0)
    def _():
        s_ref[...] = jnp.zeros_like(s_ref)

    x = x_ref[0]
    hb = _adaln_pre(x, m_ref, gpre_ref[...]).astype(BF16)
    cos, sin = cos_ref[...], sin_ref[...]
    qrs, krs, vhs, ghs = [], [], [], []
    for hd in range(RET_HEADS):
        qo = hd * RET_QK_DIM
        ko = RET_QK_WIDTH + hd * RET_QK_DIM
        vo = 2 * RET_QK_WIDTH + hd * RET_V_DIM
        go = 2 * RET_QK_WIDTH + RET_V_WIDTH + hd * RET_V_DIM
        qrs.append(_rotate(_dot(hb, win_ref[:, qo:qo + RET_QK_DIM]), cos, sin).astype(BF16))
        krs.append(_rotate(_dot(hb, win_ref[:, ko:ko + RET_QK_DIM]), cos, sin) * (RET_QK_DIM ** -0.5))
        vhs.append(_dot(hb, win_ref[:, vo:vo + RET_V_DIM]).astype(BF16))
        ghs.append(_dot(hb, win_ref[:, go:go + RET_V_DIM]))
    for c in range(tm // RET_CHUNK):
        rows = slice(c * RET_CHUNK, (c + 1) * RET_CHUNK)
        for hd in range(RET_HEADS):
            qc = qrs[hd][rows]
            kc = krs[hd][rows]
            vc = vhs[hd][rows]
            scores = _dot_nt(qc, kc.astype(BF16)) * dec_ref[hd]
            state = s_ref[0, hd]
            o = _dot(scores.astype(BF16), vc) + _dot(qc, state.astype(BF16)) * crs_ref[hd]
            kd = (kc * kdc_ref[hd]).astype(BF16)
            s_ref[0, hd] = full_ref[hd] * state + _dot_tn(kd, vc)
            obuf[rows, hd * RET_V_DIM:(hd + 1) * RET_V_DIM] = _group_norm_gate(o, ghs[hd][rows]).astype(BF16)
    y = _dot(obuf[...], wout_ref[...])
    xo_ref[0] = x + m_ref[0, 2] * _rms(y, gpost_ref[...])


def _odd_prompt(x, mod, g_pre, g_post, w_in, w_out, tm):
    b, t, _ = x.shape
    cos, sin = _rope_tables(jnp.arange(t, dtype=F32))
    decay, cross, kdec, full = _ret_tables(RET_CHUNK)
    half = RET_QK_DIM // 2
    grid_spec = pltpu.PrefetchScalarGridSpec(
        num_scalar_prefetch=1,
        grid=(b, t // tm),
        in_specs=[
            pl.BlockSpec((1, tm, D_MODEL), lambda i, j, *_: (i, j, 0)),
            pl.BlockSpec((1, 3, 1, D_MODEL), lambda i, j, *_: (i, 0, 0, 0)),
            _resident((1, D_MODEL)),
            _resident((1, D_MODEL)),
            pl.BlockSpec((tm, half), lambda i, j, *_: (j, 0)),
            pl.BlockSpec((tm, half), lambda i, j, *_: (j, 0)),
            _resident(w_in.shape),
            _resident(w_out.shape),
            _resident(decay.shape),
            _resident(cross.shape),
            _resident(kdec.shape),
        ],
        out_specs=[
            pl.BlockSpec((1, tm, D_MODEL), lambda i, j, *_: (i, j, 0)),
            pl.BlockSpec((1, RET_HEADS, RET_QK_DIM, RET_V_DIM), lambda i, j, *_: (i, 0, 0, 0)),
        ],
        scratch_shapes=[pltpu.VMEM((tm, RET_V_WIDTH), BF16)],
    )
    return pl.pallas_call(
        functools.partial(_odd_prompt_kernel, tm=tm),
        grid_spec=grid_spec,
        out_shape=[
            jax.ShapeDtypeStruct(x.shape, F32),
            jax.ShapeDtypeStruct((b, RET_HEADS, RET_QK_DIM, RET_V_DIM), F32),
        ],
        compiler_params=_params("arbitrary", "arbitrary"),
        name="odd_prompt",
    )(full, x, mod, g_pre, g_post, cos, sin, w_in, w_out, decay, cross, kdec)


def _even_sample_pre_kernel(x_ref, m_ref, gpre_ref, win_ref, pool_ref, pw_ref, ps_ref,
                            q_ref, k_ref, v_ref, pool_out_ref, opool_ref):
    hb = _adaln_pre(x_ref[...], m_ref, gpre_ref[...]).astype(BF16)
    qkvu = _dot(hb, win_ref[...])
    q_ref[...] = qkvu[:, 0:SB_WIDTH] * (SB_HEAD_DIM ** -0.5)
    k_ref[...] = qkvu[:, SB_WIDTH:2 * SB_WIDTH].T
    v_ref[...] = qkvu[:, 2 * SB_WIDTH:3 * SB_WIDTH].T
    u = qkvu[:, 3 * SB_WIDTH:]
    nh = POOL_HIST - 1
    for gi, wdw in enumerate(POOL_WINDOWS):
        lo_, hi_ = gi * POOL_GROUP, (gi + 1) * POOL_GROUP
        ug = u[:, lo_:hi_]
        ssum = ug
        for back in range(1, wdw):
            ssum = ssum + pool_ref[nh - back, :, lo_:hi_]
        d = ssum / float(wdw) - ug
        opool_ref[:, lo_:hi_] = (_dot(d.astype(BF16), pw_ref[gi]) * ps_ref[:, lo_:hi_]).astype(BF16)
    pool_out_ref[0:nh - 1] = pool_ref[1:nh]
    pool_out_ref[nh - 1] = u


def _even_sample_pre(x, mod, g_pre, w_in, pool_hist, pool_w, pool_scale):
    rows = x.shape[0]
    outs = [
        jax.ShapeDtypeStruct((rows, SB_WIDTH), F32),
        jax.ShapeDtypeStruct((SB_WIDTH, rows), F32),
        jax.ShapeDtypeStruct((SB_WIDTH, rows), F32),
        jax.ShapeDtypeStruct(pool_hist.shape, F32),
        jax.ShapeDtypeStruct((rows, POOL_WIDTH), BF16),
    ]
    return pl.pallas_call(
        _even_sample_pre_kernel,
        out_shape=outs,
        compiler_params=pltpu.CompilerParams(vmem_limit_bytes=VMEM_LIMIT_BYTES),
        name="even_sample_pre",
    )(x, mod, g_pre, w_in, pool_hist, pool_w, pool_scale)


def _sb_decode_kernel(pt_ref, bias_ref, q_ref, ck_ref, cv_ref, o_ref, kbuf, vbuf, sem, *, n_pages):
    i = pl.program_id(0)
    n = pl.num_programs(0)

    def copies(sample, slot):
        out = []
        for p in range(n_pages):
            pg = pt_ref[sample, p]
            out.append(pltpu.make_async_copy(ck_ref.at[pg], kbuf.at[slot, p], sem.at[0, slot]))
            out.append(pltpu.make_async_copy(cv_ref.at[pg], vbuf.at[slot, p], sem.at[1, slot]))
        return out

    @pl.when(i == 0)
    def _():
        for c in copies(0, 0):
            c.start()

    @pl.when(i + 1 < n)
    def _():
        for c in copies(i + 1, (i + 1) % 2):
            c.start()

    slot = i % 2
    for c in copies(i, slot):
        c.wait()

    q = q_ref[0]
    head_of_lane = lax.broadcasted_iota(jnp.int32, (SB_HEADS, SB_WIDTH), 1) // SB_HEAD_DIM
    own = head_of_lane == lax.broadcasted_iota(jnp.int32, (SB_HEADS, SB_WIDTH), 0)
    qblk = jnp.where(own, q, 0.0).astype(BF16)
    hrow = lax.broadcasted_iota(jnp.int32, (SB_HEADS, 1), 0)
    bias = jnp.zeros((SB_HEADS, 1), F32)
    for h in range(SB_HEADS):
        bias = jnp.where(hrow == h, bias_ref[h], bias)
    sps, logbs = [], []
    for p in range(n_pages):
        z = _dot(qblk, kbuf[slot, p].astype(BF16)) + bias
        sp = _softplus(z)
        sps.append(sp)
        logbs.append(z - sp)
    page = sps[0].shape[1]
    st = jnp.concatenate(sps, axis=0)
    hi = st.astype(BF16)
    lo = (st - hi.astype(F32)).astype(BF16)
    hl = jnp.concatenate([hi, lo], axis=1)
    jr = lax.broadcasted_iota(jnp.int32, (2 * page, page), 0)
    jr = jnp.where(jr >= page, jr - page, jr)
    sc = lax.broadcasted_iota(jnp.int32, (2 * page, page), 1)
    within = _dot(hl, jnp.where(jr > sc, -1.0, 0.0).astype(BF16))
    total = _dot(hl, jnp.full((2 * page, page), -1.0, BF16))
    carry = jnp.zeros((SB_HEADS, page), F32)
    o8 = jnp.zeros((SB_HEADS, SB_WIDTH), F32)
    for p in range(n_pages - 1, -1, -1):
        rows = slice(p * SB_HEADS, (p + 1) * SB_HEADS)
        w = jnp.exp(logbs[p] + within[rows] + carry).astype(BF16)
        carry = carry + total[rows]
        o8 = o8 + _dot_nt(w, vbuf[slot, p].astype(BF16))
    o_ref[0] = jnp.sum(jnp.where(own, o8, 0.0), axis=0, keepdims=True)


def _sb_decode(q, sb_bias, page_table, cache_k, cache_v):
    rows, n_pages = page_table.shape
    page = cache_k.shape[2]
    grid_spec = pltpu.PrefetchScalarGridSpec(
        num_scalar_prefetch=2,
        grid=(rows,),
        in_specs=[
            pl.BlockSpec((1, 1, SB_WIDTH), lambda i, *_: (i, 0, 0)),
            pl.BlockSpec(memory_space=pl.ANY),
            pl.BlockSpec(memory_space=pl.ANY),
        ],
        out_specs=pl.BlockSpec((1, 1, SB_WIDTH), lambda i, *_: (i, 0, 0)),
        scratch_shapes=[
            pltpu.VMEM((2, n_pages, SB_WIDTH, page), F32),
            pltpu.VMEM((2, n_pages, SB_WIDTH, page), F32),
            pltpu.SemaphoreType.DMA((2, 2)),
        ],
    )
    out = pl.pallas_call(
        functools.partial(_sb_decode_kernel, n_pages=n_pages),
        grid_spec=grid_spec,
        out_shape=jax.ShapeDtypeStruct((rows, 1, SB_WIDTH), F32),
        compiler_params=_params("arbitrary"),
        name="sb_decode",
    )(page_table, sb_bias, q.reshape(rows, 1, SB_WIDTH), cache_k, cache_v)
    return out.reshape(rows, SB_WIDTH)


def _even_sample_post_kernel(x_ref, m_ref, gpost_ref, osb_ref, opool_ref, wout_ref, xo_ref):
    y = _dot(osb_ref[...].astype(BF16), wout_ref[0:SB_WIDTH, :]) + _dot(opool_ref[...], wout_ref[SB_WIDTH:, :])
    xo_ref[...] = x_ref[...] + m_ref[0, 2] * _rms(y, gpost_ref[...])


def _even_sample_post(x, mod, g_post, o_sb, o_pool, w_out):
    return pl.pallas_call(
        _even_sample_post_kernel,
        out_shape=jax.ShapeDtypeStruct(x.shape, F32),
        compiler_params=pltpu.CompilerParams(vmem_limit_bytes=VMEM_LIMIT_BYTES),
        name="even_sample_post",
    )(x, mod, g_post, o_sb, o_pool, w_out)


def _odd_sample_pre_kernel(x_ref, m_ref, gpre_ref, cos_ref, sin_ref, win_ref, q_ref, k_ref, v_ref, g_ref):
    hb = _adaln_pre(x_ref[...], m_ref, gpre_ref[...]).astype(BF16)
    cos, sin = cos_ref[...], sin_ref[...]
    for hd in range(RET_HEADS):
        qo = hd * RET_QK_DIM
        ko = RET_QK_WIDTH + hd * RET_QK_DIM
        q_ref[:, qo:qo + RET_QK_DIM] = _rotate(_dot(hb, win_ref[:, qo:qo + RET_QK_DIM]), cos, sin)
        k_ref[:, qo:qo + RET_QK_DIM] = _rotate(_dot(hb, win_ref[:, ko:ko + RET_QK_DIM]), cos, sin) * (RET_QK_DIM ** -0.5)
    vo = 2 * RET_QK_WIDTH
    v_ref[...] = _dot(hb, win_ref[:, vo:vo + RET_V_WIDTH])
    g_ref[...] = _dot(hb, win_ref[:, vo + RET_V_WIDTH:])


def _odd_sample_pre(x, mod, g_pre, cos, sin, w_in):
    rows = x.shape[0]
    outs = [
        jax.ShapeDtypeStruct((rows, RET_QK_WIDTH), F32),
        jax.ShapeDtypeStruct((rows, RET_QK_WIDTH), F32),
        jax.ShapeDtypeStruct((rows, RET_V_WIDTH), F32),
        jax.ShapeDtypeStruct((rows, RET_V_WIDTH), F32),
    ]
    return pl.pallas_call(
        _odd_sample_pre_kernel,
        out_shape=outs,
        compiler_params=pltpu.CompilerParams(vmem_limit_bytes=VMEM_LIMIT_BYTES),
        name="odd_sample_pre",
    )(x, mod, g_pre, cos, sin, w_in)


def _ret_step_kernel(gam_ref, q_ref, k_ref, v_ref, s_ref, o_ref, so_ref, *, nb):
    row0 = lax.broadcasted_iota(jnp.int32, (16, RET_QK_DIM), 0) == 0
    for b in range(nb):
        for hd in range(RET_HEADS):
            qh = q_ref[b, :, hd * RET_QK_DIM:(hd + 1) * RET_QK_DIM]
            kh = k_ref[b, :, hd * RET_QK_DIM:(hd + 1) * RET_QK_DIM]
            vh = v_ref[b, :, hd * RET_V_DIM:(hd + 1) * RET_V_DIM]
            qb = qh.astype(BF16)
            vb = vh.astype(BF16).astype(F32)
            score = jnp.sum(qb.astype(F32) * kh.astype(BF16).astype(F32), axis=-1, keepdims=True)
            state = s_ref[b, hd]
            q16 = jnp.where(row0, jnp.broadcast_to(qh, (16, RET_QK_DIM)), 0.0).astype(BF16)
            cross = _dot(q16, state.astype(BF16))[0:1] * gam_ref[hd]
            o_ref[b, :, hd * RET_V_DIM:(hd + 1) * RET_V_DIM] = score.astype(BF16).astype(F32) * vb + cross
            kcol = jnp.broadcast_to(kh, (8, RET_QK_DIM)).T[:, 0:1]
            so_ref[b, hd] = gam_ref[hd] * state + kcol * vh


def _ret_step(q, k, v, state, nb):
    rows = q.shape[0]
    gamma = jnp.exp(jnp.asarray(_ret_log_gamma()))
    grid_spec = pltpu.PrefetchScalarGridSpec(
        num_scalar_prefetch=1,
        grid=(rows // nb,),
        in_specs=[
            pl.BlockSpec((nb, 1, RET_QK_WIDTH), lambda i, *_: (i, 0, 0)),
            pl.BlockSpec((nb, 1, RET_QK_WIDTH), lambda i, *_: (i, 0, 0)),
            pl.BlockSpec((nb, 1, RET_V_WIDTH), lambda i, *_: (i, 0, 0)),
            pl.BlockSpec((nb, RET_HEADS, RET_QK_DIM, RET_V_DIM), lambda i, *_: (i, 0, 0, 0)),
        ],
        out_specs=[
            pl.BlockSpec((nb, 1, RET_V_WIDTH), lambda i, *_: (i, 0, 0)),
            pl.BlockSpec((nb, RET_HEADS, RET_QK_DIM, RET_V_DIM), lambda i, *_: (i, 0, 0, 0)),
        ],
    )
    o, s_new = pl.pallas_call(
        functools.partial(_ret_step_kernel, nb=nb),
        grid_spec=grid_spec,
        out_shape=[
            jax.ShapeDtypeStruct((rows, 1, RET_V_WIDTH), F32),
            jax.ShapeDtypeStruct(state.shape, F32),
        ],
        compiler_params=_params("arbitrary"),
        name="retention_step",
    )(gamma, q.reshape(rows, 1, RET_QK_WIDTH), k.reshape(rows, 1, RET_QK_WIDTH), v.reshape(rows, 1, RET_V_WIDTH), state)
    return o.reshape(rows, RET_V_WIDTH), s_new


def _odd_sample_post_kernel(x_ref, m_ref, gpost_ref, o_ref, g_ref, wout_ref, xo_ref, obuf):
    for hd in range(RET_HEADS):
        cols = slice(hd * RET_V_DIM, (hd + 1) * RET_V_DIM)
        obuf[:, cols] = _group_norm_gate(o_ref[:, cols], g_ref[:, cols]).astype(BF16)
    y = _dot(obuf[...], wout_ref[...])
    xo_ref[...] = x_ref[...] + m_ref[0, 2] * _rms(y, gpost_ref[...])


def _odd_sample_post(x, mod, g_post, o, g, w_out):
    return pl.pallas_call(
        _odd_sample_post_kernel,
        out_shape=jax.ShapeDtypeStruct(x.shape, F32),
        scratch_shapes=[pltpu.VMEM((x.shape[0], RET_V_WIDTH), BF16)],
        compiler_params=pltpu.CompilerParams(vmem_limit_bytes=VMEM_LIMIT_BYTES),
        name="odd_sample_post",
    )(x, mod, g_post, o, g, w_out)


def _sample_trunk(x, mods, cache_k, cache_v, state_pool, state_ret, page_table, norm_pre, norm_post, ffn_w_in,
                  ffn_w_out, w_in_even, sb_bias, pool_w, pool_scale, w_out_even, w_in_odd, w_out_odd):
    depth = mods.shape[0]
    rows = x.shape[0]
    n_pages = page_table.shape[1]
    page = cache_k.shape[2]
    x = x.reshape(rows, D_MODEL)
    ks, vs, pools, rets = [], [], [], []
    for l in range(depth):
        li = l // 2
        mod = lambda s: mods[l, 3 * s:3 * s + 3].reshape(1, 3, rows, D_MODEL)
        gp = lambda s: norm_pre[l, s].reshape(1, D_MODEL)
        gq = lambda s: norm_post[l, s].reshape(1, D_MODEL)
        x = _ffn_sublayer(x[None], mod(0), gp(0), gq(0), ffn_w_in[l, 0], ffn_w_out[l, 0], tm=rows)[0]
        if l % 2 == 0:
            pool_hist = state_pool[li].transpose(1, 0, 2)
            q, k_t, v_t, pool_new, o_pool = _even_sample_pre(x, mod(1), gp(1), w_in_even[li], pool_hist, pool_w[li],
                                                             pool_scale[li].reshape(1, POOL_WIDTH))
            pages_t = lambda c: c[li].transpose(0, 2, 3, 1).reshape(-1, SB_WIDTH, page)
            o_sb = _sb_decode(q, sb_bias[li], page_table, pages_t(cache_k), pages_t(cache_v))
            x = _even_sample_post(x, mod(1), gq(1), o_sb, o_pool, w_out_even[li])
            heads_last = lambda a: a.reshape(SB_HEADS, SB_HEAD_DIM, rows).transpose(2, 0, 1)[:, None]
            ks.append(heads_last(k_t))
            vs.append(heads_last(v_t))
            pools.append(pool_new.transpose(1, 0, 2))
        else:
            cos, sin = _rope_tables(jnp.full((1,), n_pages * page, F32))
            q, k, v, g = _odd_sample_pre(x, mod(1), gp(1), cos, sin, w_in_odd[li])
            o, s_new = _ret_step(q, k, v, state_ret[li], nb=2)
            x = _odd_sample_post(x, mod(1), gq(1), o, g, w_out_odd[li])
            rets.append(s_new)
        x = _ffn_sublayer(x[None], mod(2), gp(2), gq(2), ffn_w_in[l, 1], ffn_w_out[l, 1], tm=rows)[0]
    return x.reshape(rows, 1, D_MODEL), jnp.stack(ks, 0), jnp.stack(vs, 0), jnp.stack(pools, 0), jnp.stack(rets, 0)


def _prompt_trunk(x, modp, norm_pre, norm_post, ffn_w_in, ffn_w_out, w_in_even, sb_bias, pool_w, pool_scale,
                  w_out_even, w_in_odd, w_out_odd):
    depth = modp.shape[0]
    b = x.shape[0]
    ks, vs, pools, rets = [], [], [], []
    for l in range(depth):
        li = l // 2
        mod = lambda s: modp[l, 3 * s:3 * s + 3].transpose(1, 0, 2).reshape(b, 3, 1, D_MODEL)
        gp = lambda s: norm_pre[l, s].reshape(1, D_MODEL)
        gq = lambda s: norm_post[l, s].reshape(1, D_MODEL)
        x = _ffn_sublayer(x, mod(0), gp(0), gq(0), ffn_w_in[l, 0], ffn_w_out[l, 0], tm=512)
        if l % 2 == 0:
            x, k, v, tail = _even_prompt(x, mod(1), gp(1), gq(1), w_in_even[li], sb_bias[li], pool_w[li],
                                         pool_scale[li].reshape(1, POOL_WIDTH), w_out_even[li], tm=256, tq=256)
            heads_last = lambda a: a.reshape(b, SB_HEADS, SB_HEAD_DIM, -1).transpose(0, 3, 1, 2)
            ks.append(heads_last(k))
            vs.append(heads_last(v))
            pools.append(tail[:, 1:])
        else:
            x, s = _odd_prompt(x, mod(1), gp(1), gq(1), w_in_odd[li], w_out_odd[li], tm=512)
            rets.append(s)
        x = _ffn_sublayer(x, mod(2), gp(2), gq(2), ffn_w_in[l, 1], ffn_w_out[l, 1], tm=512)
    return x, jnp.stack(ks, 0), jnp.stack(vs, 0), jnp.stack(pools, 0), jnp.stack(rets, 0)


def kernel(x_prompt, x_sample, c_prompt, c_sample, cache_k, cache_v, state_pool, state_ret, page_table, w_ada, b_ada, norm_pre, norm_post, ffn_w_in, ffn_w_out, w_in_even, sb_bias, pool_w, pool_scale, w_out_even, w_in_odd, w_out_odd):
    modp, mods = _modulation(c_prompt, c_sample, w_ada, b_ada)
    weights = tuple(w.astype(BF16) for w in (ffn_w_in, ffn_w_out, w_in_even))
    weights += (sb_bias, pool_w.astype(BF16), pool_scale) + tuple(w.astype(BF16) for w in (w_out_even, w_in_odd, w_out_odd))
    y_p, kp, vp, pp, rp = _prompt_trunk(x_prompt, modp, norm_pre, norm_post, *weights)
    y_s, ks, vs, ps, rs = _sample_trunk(x_sample, mods, cache_k, cache_v, state_pool, state_ret, page_table,
                                        norm_pre, norm_post, *weights)
    return (y_p, y_s, kp, vp, pp, rp, ks, vs, ps, rs)
```

```python
import functools

import numpy as np
import jax
import jax.numpy as jnp
from jax import lax
from jax.experimental import pallas as pl
from jax.experimental.pallas import tpu as pltpu

F32 = jnp.float32
BF16 = jnp.bfloat16

D_MODEL = 1024
D_FF = 2816
N_SUB = 3
NORM_EPS = 1e-6
SB_HEADS = 8
SB_HEAD_DIM = 64
SB_WIDTH = SB_HEADS * SB_HEAD_DIM
N_PAIRS = SB_HEADS // 2
POOL_WINDOWS = (2, 4, 8, 16)
POOL_GROUP = 128
POOL_WIDTH = len(POOL_WINDOWS) * POOL_GROUP
POOL_HIST = 16
RET_HEADS = 4
RET_QK_DIM = 256
RET_V_DIM = 512
RET_QK_WIDTH = RET_HEADS * RET_QK_DIM
RET_V_WIDTH = RET_HEADS * RET_V_DIM
RET_CHUNK = 128
ROPE_BASE = 10000.0
LN_EPS = 1e-5
LANES = 128
KBLK = 256
SUFFIX_PAD = 16
LOG2E = 1.4426950408889634
MASKED_LOGIT = -1e30
VMEM_LIMIT_BYTES = 56 * 1024 * 1024


def _params(*sem):
    return pltpu.CompilerParams(dimension_semantics=sem, vmem_limit_bytes=VMEM_LIMIT_BYTES)


def _resident(shape):
    nd = len(shape)
    return pl.BlockSpec(shape, lambda *_: (0,) * nd, pipeline_mode=pl.Buffered(1))


def _rms(x, g):
    ms = jnp.mean(x * x, axis=-1, keepdims=True)
    return x * lax.rsqrt(ms + NORM_EPS) * g


def _adaln_pre(x, m_ref, g):
    return _rms(x, g) * (1.0 + m_ref[0, 1]) + m_ref[0, 0]


def _silu(x):
    return x * jax.nn.sigmoid(x)


def _dot(a, b):
    return jnp.dot(a, b, preferred_element_type=F32)


def _dot_nt(a, b):
    return lax.dot_general(a, b, (((1,), (1,)), ((), ())), preferred_element_type=F32)


def _dot_tn(a, b):
    return lax.dot_general(a, b, (((0,), (0,)), ((), ())), preferred_element_type=F32)


def _mod_kernel(cp_ref, cs_ref, w_ref, b_ref, mp_ref, ms_ref):
    w = w_ref[0].astype(BF16)
    b = b_ref[0, 0]
    mp_ref[0, 0] = _dot(_silu(cp_ref[...]).astype(BF16), w) + b
    ms_ref[0, 0] = _dot(_silu(cs_ref[...]).astype(BF16), w) + b


def _modulation(c_p, c_s, w_ada, b_ada):
    depth = w_ada.shape[0]
    nb = N_SUB * 3
    bp, bs = c_p.shape[0], c_s.shape[0]
    return pl.pallas_call(
        _mod_kernel,
        grid=(depth, nb),
        in_specs=[
            pl.BlockSpec((bp, D_MODEL), lambda l, n: (0, 0)),
            pl.BlockSpec((bs, D_MODEL), lambda l, n: (0, 0)),
            pl.BlockSpec((1, D_MODEL, D_MODEL), lambda l, n: (l, 0, n)),
            pl.BlockSpec((1, 1, 1, D_MODEL), lambda l, n: (l, n, 0, 0)),
        ],
        out_specs=[
            pl.BlockSpec((1, 1, bp, D_MODEL), lambda l, n: (l, n, 0, 0)),
            pl.BlockSpec((1, 1, bs, D_MODEL), lambda l, n: (l, n, 0, 0)),
        ],
        out_shape=[
            jax.ShapeDtypeStruct((depth, nb, bp, D_MODEL), F32),
            jax.ShapeDtypeStruct((depth, nb, bs, D_MODEL), F32),
        ],
        compiler_params=_params("arbitrary", "arbitrary"),
        name="adaln_modulation",
    )(c_p, c_s, w_ada, b_ada.reshape(depth, nb, 1, D_MODEL))


def _ffn_kernel(x_ref, m_ref, gpre_ref, gpost_ref, win_ref, wout_ref, o_ref):
    x = x_ref[0]
    hb = _adaln_pre(x, m_ref, gpre_ref[...]).astype(BF16)
    gu = _dot(hb, win_ref[...])
    a = (_silu(gu[:, :D_FF]) * gu[:, D_FF:]).astype(BF16)
    y = _dot(a, wout_ref[...])
    o_ref[0] = x + 0.5 * m_ref[0, 2] * _rms(y, gpost_ref[...])


def _ffn_sublayer(x, mod, g_pre, g_post, w_in, w_out, tm):
    g, t, _ = x.shape
    r = mod.shape[2]
    rb = 1 if r == 1 else tm
    return pl.pallas_call(
        _ffn_kernel,
        grid=(g, t // tm),
        in_specs=[
            pl.BlockSpec((1, tm, D_MODEL), lambda b, j: (b, j, 0)),
            pl.BlockSpec((1, 3, rb, D_MODEL), (lambda b, j: (b, 0, 0, 0)) if r == 1 else (lambda b, j: (b, 0, j, 0))),
            _resident((1, D_MODEL)),
            _resident((1, D_MODEL)),
            _resident((D_MODEL, 2 * D_FF)),
            _resident((D_FF, D_MODEL)),
        ],
        out_specs=pl.BlockSpec((1, tm, D_MODEL), lambda b, j: (b, j, 0)),
        out_shape=jax.ShapeDtypeStruct(x.shape, F32),
        compiler_params=_params("arbitrary", "arbitrary"),
        name="ffn_sublayer",
    )(x, mod, g_pre, g_post, w_in, w_out)


def _softplus(z):
    return jnp.maximum(z, 0.0) + jnp.log(1.0 + jnp.exp(-jnp.abs(z)))


def _even_prompt_kernel(bias_ref, x_ref, m_ref, gpre_ref, gpost_ref, win_ref, pw_ref, ps_ref, wout_ref,
                        xo_ref, k_ref, v_ref, tail_ref,
                        kbuf, vtbuf, ubuf, obuf, acc_ref, car_ref, *, tm, tq):
    j = pl.program_id(1)
    nj = pl.num_programs(1)
    x = x_ref[0]
    hb = _adaln_pre(x, m_ref, gpre_ref[...]).astype(BF16)
    qk = _dot(hb, win_ref[:, 0:2 * SB_WIDTH])
    q = qk[:, 0:SB_WIDTH] * (SB_HEAD_DIM ** -0.5)
    k = qk[:, SB_WIDTH:]
    nsub = tm // KBLK
    for s in range(nsub):
        kbuf[j * nsub + s] = k[s * KBLK:(s + 1) * KBLK].astype(BF16)

    @pl.when(j == 0)
    def _():
        ubuf[0:POOL_HIST] = jnp.zeros((POOL_HIST, POOL_WIDTH), F32)

    def project_values():
        k_ref[0] = k.T
        vu = _dot(hb, win_ref[:, 2 * SB_WIDTH:])
        v_t = vu[:, 0:SB_WIDTH].T
        v_ref[0] = v_t
        for s in range(nsub):
            vtbuf[j * nsub + s] = v_t[:, s * KBLK:(s + 1) * KBLK].astype(BF16)
        ubuf[POOL_HIST:POOL_HIST + tm] = vu[:, SB_WIDTH:]

    lane = lax.broadcasted_iota(jnp.int32, (tq, 2 * SB_HEAD_DIM), 1)
    krow = lax.broadcasted_iota(jnp.int32, (KBLK, 2 * tq), 0)
    qcol = lax.broadcasted_iota(jnp.int32, (KBLK, 2 * tq), 1)
    qcol = jnp.where(qcol >= tq, qcol - tq, qcol)
    first_head = lax.broadcasted_iota(jnp.int32, (1, 2 * tq), 1) < tq
    mr = lax.broadcasted_iota(jnp.int32, (KBLK + SUFFIX_PAD, 2 * KBLK), 0)
    mc = lax.broadcasted_iota(jnp.int32, (KBLK + SUFFIX_PAD, 2 * KBLK), 1)
    mc = jnp.where(mc >= KBLK, mc - KBLK, mc)
    suffix_neg = jnp.where((mc > mr) | (mr == KBLK), -1.0, 0.0).astype(BF16)

    for qt in range(tm // tq):
        qtile = j * (tm // tq) + qt
        qrows = q[qt * tq:(qt + 1) * tq]
        qs_t, bias_row = [], []
        for p in range(N_PAIRS):
            q2 = qrows[:, 128 * p:128 * (p + 1)]
            qa = jnp.where(lane < SB_HEAD_DIM, q2, 0.0).T
            qb = jnp.where(lane >= SB_HEAD_DIM, q2, 0.0).T
            qs_t.append(jnp.concatenate([qa, qb], axis=1).astype(BF16))
            bias_row.append(jnp.where(first_head, bias_ref[2 * p], bias_ref[2 * p + 1]) * LOG2E)
        acc_ref[...] = jnp.zeros_like(acc_ref)
        car_ref[...] = jnp.zeros_like(car_ref)

        def step(kb, masked, after_scores=None):
            kblk = kbuf[kb]
            zs = [_dot(kblk[:, 128 * p:128 * (p + 1)], qs_t[p]) for p in range(N_PAIRS)]
            if after_scores is not None:
                after_scores()
            if masked:
                hide = jnp.where((kb * KBLK + krow) < (qtile * tq + qcol), 0.0, MASKED_LOGIT)
            logbs, hls = [], []
            for p in range(N_PAIRS):
                z = zs[p] * LOG2E + bias_row[p]
                if masked:
                    z = z + hide
                sp = jnp.maximum(z, 0.0) + jnp.log2(1.0 + jnp.exp2(-jnp.abs(z)))
                logbs.append(z - sp)
                hi = sp.astype(BF16)
                lo = (sp - hi.astype(F32)).astype(BF16)
                hls.append(jnp.concatenate([hi, lo], axis=0))
            sums = [_dot(suffix_neg, hls[p]) for p in range(N_PAIRS)]
            ws = []
            for p in range(N_PAIRS):
                ws.append(jnp.exp2(logbs[p] + sums[p][0:KBLK] + car_ref[p]).astype(BF16))
                car_ref[p] += sums[p][KBLK:KBLK + 1]
            vtb = vtbuf[kb]
            for p in range(N_PAIRS):
                acc_ref[p] += _dot(vtb[128 * p:128 * (p + 1), :], ws[p])

        ndiag = tq // KBLK
        n_full = qtile * ndiag
        for dk in range(ndiag - 1, -1, -1):
            step(n_full + dk, True, project_values if dk == ndiag - 1 else None)

        def body(it, carry):
            step(n_full - 1 - it, False)
            return carry

        lax.fori_loop(0, n_full, body, 0)

        for p in range(N_PAIRS):
            a = acc_ref[p]
            o_t = jnp.concatenate([a[0:SB_HEAD_DIM, 0:tq], a[SB_HEAD_DIM:, tq:]], axis=0)
            obuf[qt * tq:(qt + 1) * tq, 128 * p:128 * (p + 1)] = o_t.T.astype(BF16)

    y = _dot(obuf[:, 0:SB_WIDTH], wout_ref[0:SB_WIDTH, :])

    pos1 = (j * tm + 1 + lax.broadcasted_iota(jnp.int32, (tm, POOL_GROUP), 0)).astype(F32)
    for gi, wdw in enumerate(POOL_WINDOWS):
        lo_, hi_ = gi * POOL_GROUP, (gi + 1) * POOL_GROUP
        ug = ubuf[POOL_HIST:POOL_HIST + tm, lo_:hi_]
        ssum = ug
        for sft in range(1, wdw):
            ssum = ssum + ubuf[POOL_HIST - sft:POOL_HIST - sft + tm, lo_:hi_]
        d = ssum / jnp.minimum(pos1, float(wdw)) - ug
        yg = _dot(d.astype(BF16), pw_ref[gi]) * ps_ref[:, lo_:hi_]
        obuf[:, SB_WIDTH + lo_:SB_WIDTH + hi_] = yg.astype(BF16)
    hist = ubuf[tm:tm + POOL_HIST]
    ubuf[0:POOL_HIST] = hist

    @pl.when(j == nj - 1)
    def _():
        tail_ref[0] = hist

    y = y + _dot(obuf[:, SB_WIDTH:], wout_ref[SB_WIDTH:, :])
    xo_ref[0] = x + m_ref[0, 2] * _rms(y, gpost_ref[...])


def _even_prompt(x, mod, g_pre, g_post, w_in, sb_bias, pool_w, pool_scale, w_out, tm, tq):
    assert tm == tq and tq % KBLK == 0
    b, t, _ = x.shape
    nkb = t // KBLK
    grid_spec = pltpu.PrefetchScalarGridSpec(
        num_scalar_prefetch=1,
        grid=(b, t // tm),
        in_specs=[
            pl.BlockSpec((1, tm, D_MODEL), lambda i, j, *_: (i, j, 0)),
            pl.BlockSpec((1, 3, 1, D_MODEL), lambda i, j, *_: (i, 0, 0, 0)),
            _resident((1, D_MODEL)),
            _resident((1, D_MODEL)),
            _resident(w_in.shape),
            _resident(pool_w.shape),
            _resident((1, POOL_WIDTH)),
            _resident(w_out.shape),
        ],
        out_specs=[
            pl.BlockSpec((1, tm, D_MODEL), lambda i, j, *_: (i, j, 0)),
            pl.BlockSpec((1, SB_WIDTH, tm), lambda i, j, *_: (i, 0, j)),
            pl.BlockSpec((1, SB_WIDTH, tm), lambda i, j, *_: (i, 0, j)),
            pl.BlockSpec((1, POOL_HIST, POOL_WIDTH), lambda i, j, *_: (i, 0, 0)),
        ],
        scratch_shapes=[
            pltpu.VMEM((nkb, KBLK, SB_WIDTH), BF16),
            pltpu.VMEM((nkb, SB_WIDTH, KBLK), BF16),
            pltpu.VMEM((POOL_HIST + tm, POOL_WIDTH), F32),
            pltpu.VMEM((tm, SB_WIDTH + POOL_WIDTH), BF16),
            pltpu.VMEM((N_PAIRS, 2 * SB_HEAD_DIM, 2 * tq), F32),
            pltpu.VMEM((N_PAIRS, 1, 2 * tq), F32),
        ],
    )
    return pl.pallas_call(
        functools.partial(_even_prompt_kernel, tm=tm, tq=tq),
        grid_spec=grid_spec,
        out_shape=[
            jax.ShapeDtypeStruct(x.shape, F32),
            jax.ShapeDtypeStruct((b, SB_WIDTH, t), F32),
            jax.ShapeDtypeStruct((b, SB_WIDTH, t), F32),
            jax.ShapeDtypeStruct((b, POOL_HIST, POOL_WIDTH), F32),
        ],
        compiler_params=_params("arbitrary", "arbitrary"),
        name="even_prompt",
    )(sb_bias, x, mod, g_pre, g_post, w_in, pool_w, pool_scale, w_out)


def _ret_log_gamma():
    return np.log(np.float32(1.0) - np.exp2(np.float32(-5.0) - np.arange(RET_HEADS, dtype=np.float32))).astype(np.float32)


def _ret_tables(chunk):
    lg = jnp.asarray(_ret_log_gamma())
    i = jnp.arange(chunk, dtype=F32)
    diff = i[:, None] - i[None, :]
    decay = jnp.where(diff[None] >= 0, jnp.exp(jnp.maximum(diff, 0.0)[None] * lg[:, None, None]), 0.0)
    cross = jnp.exp((i[None, :] + 1.0) * lg[:, None])[:, :, None]
    kdec = jnp.exp((chunk - 1.0 - i)[None, :] * lg[:, None])[:, :, None]
    full = jnp.exp(chunk * lg)
    return decay, cross, kdec, full


def _rope_tables(pos):
    half = RET_QK_DIM // 2
    inv = ROPE_BASE ** (-jnp.arange(half, dtype=F32) / half)
    ang = pos[:, None] * inv[None, :]
    return jnp.cos(ang), jnp.sin(ang)


def _rotate(x, cos, sin):
    half = RET_QK_DIM // 2
    x1, x2 = x[:, :half], x[:, half:]
    return jnp.concatenate([x1 * cos - x2 * sin, x1 * sin + x2 * cos], axis=-1)


def _group_norm_gate(o, g):
    mu = jnp.mean(o, axis=-1, keepdims=True)
    c = o - mu
    var = jnp.mean(c * c, axis=-1, keepdims=True)
    return c * lax.rsqrt(var + LN_EPS) * _silu(g)


def _odd_prompt_kernel(full_ref, x_ref, m_ref, gpre_ref, gpost_ref, cos_ref, sin_ref, win_ref, wout_ref,
                       dec_ref, crs_ref, kdc_ref, xo_ref, s_ref, obuf, *, tm):
    j = pl.program_id(1)

    @pl.when(j == 0)
    def _():
        s_ref[...] = jnp.zeros_like(s_ref)

    x = x_ref[0]
    hb = _adaln_pre(x, m_ref, gpre_ref[...]).astype(BF16)
    cos, sin = cos_ref[...], sin_ref[...]
    qrs, krs, vhs, ghs = [], [], [], []
    for hd in range(RET_HEADS):
        qo = hd * RET_QK_DIM
        ko = RET_QK_WIDTH + hd * RET_QK_DIM
        vo = 2 * RET_QK_WIDTH + hd * RET_V_DIM
        go = 2 * RET_QK_WIDTH + RET_V_WIDTH + hd * RET_V_DIM
        qrs.append(_rotate(_dot(hb, win_ref[:, qo:qo + RET_QK_DIM]), cos, sin).astype(BF16))
        krs.append(_rotate(_dot(hb, win_ref[:, ko:ko + RET_QK_DIM]), cos, sin) * (RET_QK_DIM ** -0.5))
        vhs.append(_dot(hb, win_ref[:, vo:vo + RET_V_DIM]).astype(BF16))
        ghs.append(_dot(hb, win_ref[:, go:go + RET_V_DIM]))
    for c in range(tm // RET_CHUNK):
        rows = slice(c * RET_CHUNK, (c + 1) * RET_CHUNK)
        for hd in range(RET_HEADS):
            qc = qrs[hd][rows]
            kc = krs[hd][rows]
            vc = vhs[hd][rows]
            scores = _dot_nt(qc, kc.astype(BF16)) * dec_ref[hd]
            state = s_ref[0, hd]
            o = _dot(scores.astype(BF16), vc) + _dot(qc, state.astype(BF16)) * crs_ref[hd]
            kd = (kc * kdc_ref[hd]).astype(BF16)
            s_ref[0, hd] = full_ref[hd] * state + _dot_tn(kd, vc)
            obuf[rows, hd * RET_V_DIM:(hd + 1) * RET_V_DIM] = _group_norm_gate(o, ghs[hd][rows]).astype(BF16)
    y = _dot(obuf[...], wout_ref[...])
    xo_ref[0] = x + m_ref[0, 2] * _rms(y, gpost_ref[...])


def _odd_prompt(x, mod, g_pre, g_post, w_in, w_out, tm):
    b, t, _ = x.shape
    cos, sin = _rope_tables(jnp.arange(t, dtype=F32))
    decay, cross, kdec, full = _ret_tables(RET_CHUNK)
    half = RET_QK_DIM // 2
    grid_spec = pltpu.PrefetchScalarGridSpec(
        num_scalar_prefetch=1,
        grid=(b, t // tm),
        in_specs=[
            pl.BlockSpec((1, tm, D_MODEL), lambda i, j, *_: (i, j, 0)),
            pl.BlockSpec((1, 3, 1, D_MODEL), lambda i, j, *_: (i, 0, 0, 0)),
            _resident((1, D_MODEL)),
            _resident((1, D_MODEL)),
            pl.BlockSpec((tm, half), lambda i, j, *_: (j, 0)),
            pl.BlockSpec((tm, half), lambda i, j, *_: (j, 0)),
            _resident(w_in.shape),
            _resident(w_out.shape),
            _resident(decay.shape),
            _resident(cross.shape),
            _resident(kdec.shape),
        ],
        out_specs=[
            pl.BlockSpec((1, tm, D_MODEL), lambda i, j, *_: (i, j, 0)),
            pl.BlockSpec((1, RET_HEADS, RET_QK_DIM, RET_V_DIM), lambda i, j, *_: (i, 0, 0, 0)),
        ],
        scratch_shapes=[pltpu.VMEM((tm, RET_V_WIDTH), BF16)],
    )
    return pl.pallas_call(
        functools.partial(_odd_prompt_kernel, tm=tm),
        grid_spec=grid_spec,
        out_shape=[
            jax.ShapeDtypeStruct(x.shape, F32),
            jax.ShapeDtypeStruct((b, RET_HEADS, RET_QK_DIM, RET_V_DIM), F32),
        ],
        compiler_params=_params("arbitrary", "arbitrary"),
        name="odd_prompt",
    )(full, x, mod, g_pre, g_post, cos, sin, w_in, w_out, decay, cross, kdec)


def _even_sample_pre_kernel(x_ref, m_ref, gpre_ref, win_ref, pool_ref, pw_ref, ps_ref,
                            q_ref, k_ref, v_ref, pool_out_ref, opool_ref):
    hb = _adaln_pre(x_ref[...], m_ref, gpre_ref[...]).astype(BF16)
    qkvu = _dot(hb, win_ref[...])
    q_ref[...] = qkvu[:, 0:SB_WIDTH] * (SB_HEAD_DIM ** -0.5)
    k_ref[...] = qkvu[:, SB_WIDTH:2 * SB_WIDTH].T
    v_ref[...] = qkvu[:, 2 * SB_WIDTH:3 * SB_WIDTH].T
    u = qkvu[:, 3 * SB_WIDTH:]
    nh = POOL_HIST - 1
    for gi, wdw in enumerate(POOL_WINDOWS):
        lo_, hi_ = gi * POOL_GROUP, (gi + 1) * POOL_GROUP
        ug = u[:, lo_:hi_]
        ssum = ug
        for back in range(1, wdw):
            ssum = ssum + pool_ref[nh - back, :, lo_:hi_]
        d = ssum / float(wdw) - ug
        opool_ref[:, lo_:hi_] = (_dot(d.astype(BF16), pw_ref[gi]) * ps_ref[:, lo_:hi_]).astype(BF16)
    pool_out_ref[0:nh - 1] = pool_ref[1:nh]
    pool_out_ref[nh - 1] = u


def _even_sample_pre(x, mod, g_pre, w_in, pool_hist, pool_w, pool_scale):
    rows = x.shape[0]
    outs = [
        jax.ShapeDtypeStruct((rows, SB_WIDTH), F32),
        jax.ShapeDtypeStruct((SB_WIDTH, rows), F32),
        jax.ShapeDtypeStruct((SB_WIDTH, rows), F32),
        jax.ShapeDtypeStruct(pool_hist.shape, F32),
        jax.ShapeDtypeStruct((rows, POOL_WIDTH), BF16),
    ]
    return pl.pallas_call(
        _even_sample_pre_kernel,
        out_shape=outs,
        compiler_params=pltpu.CompilerParams(vmem_limit_bytes=VMEM_LIMIT_BYTES),
        name="even_sample_pre",
    )(x, mod, g_pre, w_in, pool_hist, pool_w, pool_scale)


def _sb_decode_kernel(pt_ref, bias_ref, q_ref, ck_ref, cv_ref, o_ref, kbuf, vbuf, sem, *, n_pages):
    i = pl.program_id(0)
    n = pl.num_programs(0)

    def copies(sample, slot):
        out = []
        for p in range(n_pages):
            pg = pt_ref[sample, p]
            out.append(pltpu.make_async_copy(ck_ref.at[pg], kbuf.at[slot, p], sem.at[0, slot]))
            out.append(pltpu.make_async_copy(cv_ref.at[pg], vbuf.at[slot, p], sem.at[1, slot]))
        return out

    @pl.when(i == 0)
    def _():
        for c in copies(0, 0):
            c.start()

    @pl.when(i + 1 < n)
    def _():
        for c in copies(i + 1, (i + 1) % 2):
            c.start()

    slot = i % 2
    for c in copies(i, slot):
        c.wait()

    q = q_ref[0]
    head_of_lane = lax.broadcasted_iota(jnp.int32, (SB_HEADS, SB_WIDTH), 1) // SB_HEAD_DIM
    own = head_of_lane == lax.broadcasted_iota(jnp.int32, (SB_HEADS, SB_WIDTH), 0)
    qblk = jnp.where(own, q, 0.0).astype(BF16)
    hrow = lax.broadcasted_iota(jnp.int32, (SB_HEADS, 1), 0)
    bias = jnp.zeros((SB_HEADS, 1), F32)
    for h in range(SB_HEADS):
        bias = jnp.where(hrow == h, bias_ref[h], bias)
    sps, logbs = [], []
    for p in range(n_pages):
        z = _dot(qblk, kbuf[slot, p].astype(BF16)) + bias
        sp = _softplus(z)
        sps.append(sp)
        logbs.append(z - sp)
    page = sps[0].shape[1]
    st = jnp.concatenate(sps, axis=0)
    hi = st.astype(BF16)
    lo = (st - hi.astype(F32)).astype(BF16)
    hl = jnp.concatenate([hi, lo], axis=1)
    jr = lax.broadcasted_iota(jnp.int32, (2 * page, page), 0)
    jr = jnp.where(jr >= page, jr - page, jr)
    sc = lax.broadcasted_iota(jnp.int32, (2 * page, page), 1)
    within = _dot(hl, jnp.where(jr > sc, -1.0, 0.0).astype(BF16))
    total = _dot(hl, jnp.full((2 * page, page), -1.0, BF16))
    carry = jnp.zeros((SB_HEADS, page), F32)
    o8 = jnp.zeros((SB_HEADS, SB_WIDTH), F32)
    for p in range(n_pages - 1, -1, -1):
        rows = slice(p * SB_HEADS, (p + 1) * SB_HEADS)
        w = jnp.exp(logbs[p] + within[rows] + carry).astype(BF16)
        carry = carry + total[rows]
        o8 = o8 + _dot_nt(w, vbuf[slot, p].astype(BF16))
    o_ref[0] = jnp.sum(jnp.where(own, o8, 0.0), axis=0, keepdims=True)


def _sb_decode(q, sb_bias, page_table, cache_k, cache_v):
    rows, n_pages = page_table.shape
    page = cache_k.shape[2]
    grid_spec = pltpu.PrefetchScalarGridSpec(
        num_scalar_prefetch=2,
        grid=(rows,),
        in_specs=[
            pl.BlockSpec((1, 1, SB_WIDTH), lambda i, *_: (i, 0, 0)),
            pl.BlockSpec(memory_space=pl.ANY),
            pl.BlockSpec(memory_space=pl.ANY),
        ],
        out_specs=pl.BlockSpec((1, 1, SB_WIDTH), lambda i, *_: (i, 0, 0)),
        scratch_shapes=[
            pltpu.VMEM((2, n_pages, SB_WIDTH, page), F32),
            pltpu.VMEM((2, n_pages, SB_WIDTH, page), F32),
            pltpu.SemaphoreType.DMA((2, 2)),
        ],
    )
    out = pl.pallas_call(
        functools.partial(_sb_decode_kernel, n_pages=n_pages),
        grid_spec=grid_spec,
        out_shape=jax.ShapeDtypeStruct((rows, 1, SB_WIDTH), F32),
        compiler_params=_params("arbitrary"),
        name="sb_decode",
    )(page_table, sb_bias, q.reshape(rows, 1, SB_WIDTH), cache_k, cache_v)
    return out.reshape(rows, SB_WIDTH)


def _even_sample_post_kernel(x_ref, m_ref, gpost_ref, osb_ref, opool_ref, wout_ref, xo_ref):
    y = _dot(osb_ref[...].astype(BF16), wout_ref[0:SB_WIDTH, :]) + _dot(opool_ref[...], wout_ref[SB_WIDTH:, :])
    xo_ref[...] = x_ref[...] + m_ref[0, 2] * _rms(y, gpost_ref[...])


def _even_sample_post(x, mod, g_post, o_sb, o_pool, w_out):
    return pl.pallas_call(
        _even_sample_post_kernel,
        out_shape=jax.ShapeDtypeStruct(x.shape, F32),
        compiler_params=pltpu.CompilerParams(vmem_limit_bytes=VMEM_LIMIT_BYTES),
        name="even_sample_post",
    )(x, mod, g_post, o_sb, o_pool, w_out)


def _odd_sample_pre_kernel(x_ref, m_ref, gpre_ref, cos_ref, sin_ref, win_ref, q_ref, k_ref, v_ref, g_ref):
    hb = _adaln_pre(x_ref[...], m_ref, gpre_ref[...]).astype(BF16)
    cos, sin = cos_ref[...], sin_ref[...]
    for hd in range(RET_HEADS):
        qo = hd * RET_QK_DIM
        ko = RET_QK_WIDTH + hd * RET_QK_DIM
        q_ref[:, qo:qo + RET_QK_DIM] = _rotate(_dot(hb, win_ref[:, qo:qo + RET_QK_DIM]), cos, sin)
        k_ref[:, qo:qo + RET_QK_DIM] = _rotate(_dot(hb, win_ref[:, ko:ko + RET_QK_DIM]), cos, sin) * (RET_QK_DIM ** -0.5)
    vo = 2 * RET_QK_WIDTH
    v_ref[...] = _dot(hb, win_ref[:, vo:vo + RET_V_WIDTH])
    g_ref[...] = _dot(hb, win_ref[:, vo + RET_V_WIDTH:])


def _odd_sample_pre(x, mod, g_pre, cos, sin, w_in):
    rows = x.shape[0]
    outs = [
        jax.ShapeDtypeStruct((rows, RET_QK_WIDTH), F32),
        jax.ShapeDtypeStruct((rows, RET_QK_WIDTH), F32),
        jax.ShapeDtypeStruct((rows, RET_V_WIDTH), F32),
        jax.ShapeDtypeStruct((rows, RET_V_WIDTH), F32),
    ]
    return pl.pallas_call(
        _odd_sample_pre_kernel,
        out_shape=outs,
        compiler_params=pltpu.CompilerParams(vmem_limit_bytes=VMEM_LIMIT_BYTES),
        name="odd_sample_pre",
    )(x, mod, g_pre, cos, sin, w_in)


def _ret_step_kernel(gam_ref, q_ref, k_ref, v_ref, s_ref, o_ref, so_ref, *, nb):
    row0 = lax.broadcasted_iota(jnp.int32, (16, RET_QK_DIM), 0) == 0
    for b in range(nb):
        for hd in range(RET_HEADS):
            qh = q_ref[b, :, hd * RET_QK_DIM:(hd + 1) * RET_QK_DIM]
            kh = k_ref[b, :, hd * RET_QK_DIM:(hd + 1) * RET_QK_DIM]
            vh = v_ref[b, :, hd * RET_V_DIM:(hd + 1) * RET_V_DIM]
            qb = qh.astype(BF16)
            vb = vh.astype(BF16).astype(F32)
            score = jnp.sum(qb.astype(F32) * kh.astype(BF16).astype(F32), axis=-1, keepdims=True)
            state = s_ref[b, hd]
            q16 = jnp.where(row0, jnp.broadcast_to(qh, (16, RET_QK_DIM)), 0.0).astype(BF16)
            cross = _dot(q16, state.astype(BF16))[0:1] * gam_ref[hd]
            o_ref[b, :, hd * RET_V_DIM:(hd + 1) * RET_V_DIM] = score.astype(BF16).astype(F32) * vb + cross
            kcol = jnp.broadcast_to(kh, (8, RET_QK_DIM)).T[:, 0:1]
            so_ref[b, hd] = gam_ref[hd] * state + kcol * vh


def _ret_step(q, k, v, state, nb):
    rows = q.shape[0]
    gamma = jnp.exp(jnp.asarray(_ret_log_gamma()))
    grid_spec = pltpu.PrefetchScalarGridSpec(
        num_scalar_prefetch=1,
        grid=(rows // nb,),
        in_specs=[
            pl.BlockSpec((nb, 1, RET_QK_WIDTH), lambda i, *_: (i, 0, 0)),
            pl.BlockSpec((nb, 1, RET_QK_WIDTH), lambda i, *_: (i, 0, 0)),
            pl.BlockSpec((nb, 1, RET_V_WIDTH), lambda i, *_: (i, 0, 0)),
            pl.BlockSpec((nb, RET_HEADS, RET_QK_DIM, RET_V_DIM), lambda i, *_: (i, 0, 0, 0)),
        ],
        out_specs=[
            pl.BlockSpec((nb, 1, RET_V_WIDTH), lambda i, *_: (i, 0, 0)),
            pl.BlockSpec((nb, RET_HEADS, RET_QK_DIM, RET_V_DIM), lambda i, *_: (i, 0, 0, 0)),
        ],
    )
    o, s_new = pl.pallas_call(
        functools.partial(_ret_step_kernel, nb=nb),
        grid_spec=grid_spec,
        out_shape=[
            jax.ShapeDtypeStruct((rows, 1, RET_V_WIDTH), F32),
            jax.ShapeDtypeStruct(state.shape, F32),
        ],
        compiler_params=_params("arbitrary"),
        name="retention_step",
    )(gamma, q.reshape(rows, 1, RET_QK_WIDTH), k.reshape(rows, 1, RET_QK_WIDTH), v.reshape(rows, 1, RET_V_WIDTH), state)
    return o.reshape(rows, RET_V_WIDTH), s_new


def _odd_sample_post_kernel(x_ref, m_ref, gpost_ref, o_ref, g_ref, wout_ref, xo_ref, obuf):
    for hd in range(RET_HEADS):
        cols = slice(hd * RET_V_DIM, (hd + 1) * RET_V_DIM)
        obuf[:, cols] = _group_norm_gate(o_ref[:, cols], g_ref[:, cols]).astype(BF16)
    y = _dot(obuf[...], wout_ref[...])
    xo_ref[...] = x_ref[...] + m_ref[0, 2] * _rms(y, gpost_ref[...])


def _odd_sample_post(x, mod, g_post, o, g, w_out):
    return pl.pallas_call(
        _odd_sample_post_kernel,
        out_shape=jax.ShapeDtypeStruct(x.shape, F32),
        scratch_shapes=[pltpu.VMEM((x.shape[0], RET_V_WIDTH), BF16)],
        compiler_params=pltpu.CompilerParams(vmem_limit_bytes=VMEM_LIMIT_BYTES),
        name="odd_sample_post",
    )(x, mod, g_post, o, g, w_out)


def _sample_trunk(x, mods, cache_k, cache_v, state_pool, state_ret, page_table, norm_pre, norm_post, ffn_w_in,
                  ffn_w_out, w_in_even, sb_bias, pool_w, pool_scale, w_out_even, w_in_odd, w_out_odd):
    depth = mods.shape[0]
    rows = x.shape[0]
    n_pages = page_table.shape[1]
    page = cache_k.shape[2]
    x = x.reshape(rows, D_MODEL)
    ks, vs, pools, rets = [], [], [], []
    for l in range(depth):
        li = l // 2
        mod = lambda s: mods[l, 3 * s:3 * s + 3].reshape(1, 3, rows, D_MODEL)
        gp = lambda s: norm_pre[l, s].reshape(1, D_MODEL)
        gq = lambda s: norm_post[l, s].reshape(1, D_MODEL)
        x = _ffn_sublayer(x[None], mod(0), gp(0), gq(0), ffn_w_in[l][0], ffn_w_out[l][0], tm=rows)[0]
        if l % 2 == 0:
            pool_hist = state_pool[li].transpose(1, 0, 2)
            q, k_t, v_t, pool_new, o_pool = _even_sample_pre(x, mod(1), gp(1), w_in_even[li], pool_hist, pool_w[li],
                                                             pool_scale[li].reshape(1, POOL_WIDTH))
            pages_t = lambda c: c[li].transpose(0, 2, 3, 1).reshape(-1, SB_WIDTH, page)
            o_sb = _sb_decode(q, sb_bias[li], page_table, pages_t(cache_k), pages_t(cache_v))
            x = _even_sample_post(x, mod(1), gq(1), o_sb, o_pool, w_out_even[li])
            heads_last = lambda a: a.reshape(SB_HEADS, SB_HEAD_DIM, rows).transpose(2, 0, 1)[:, None]
            ks.append(heads_last(k_t))
            vs.append(heads_last(v_t))
            pools.append(pool_new.transpose(1, 0, 2))
        else:
            cos, sin = _rope_tables(jnp.full((1,), n_pages * page, F32))
            q, k, v, g = _odd_sample_pre(x, mod(1), gp(1), cos, sin, w_in_odd[li])
            o, s_new = _ret_step(q, k, v, state_ret[li], nb=2)
            x = _odd_sample_post(x, mod(1), gq(1), o, g, w_out_odd[li])
            rets.append(s_new)
        x = _ffn_sublayer(x[None], mod(2), gp(2), gq(2), ffn_w_in[l][1], ffn_w_out[l][1], tm=rows)[0]
    return x.reshape(rows, 1, D_MODEL), jnp.stack(ks, 0), jnp.stack(vs, 0), jnp.stack(pools, 0), jnp.stack(rets, 0)


def _prompt_trunk(x, modp, norm_pre, norm_post, ffn_w_in, ffn_w_out, w_in_even, sb_bias, pool_w, pool_scale,
                  w_out_even, w_in_odd, w_out_odd):
    depth = modp.shape[0]
    b = x.shape[0]
    ks, vs, pools, rets = [], [], [], []
    for l in range(depth):
        li = l // 2
        mod = lambda s: modp[l, 3 * s:3 * s + 3].transpose(1, 0, 2).reshape(b, 3, 1, D_MODEL)
        gp = lambda s: norm_pre[l, s].reshape(1, D_MODEL)
        gq = lambda s: norm_post[l, s].reshape(1, D_MODEL)
        x = _ffn_sublayer(x, mod(0), gp(0), gq(0), ffn_w_in[l][0], ffn_w_out[l][0], tm=512)
        if l % 2 == 0:
            x, k, v, tail = _even_prompt(x, mod(1), gp(1), gq(1), w_in_even[li], sb_bias[li], pool_w[li],
                                         pool_scale[li].reshape(1, POOL_WIDTH), w_out_even[li], tm=256, tq=256)
            heads_last = lambda a: a.reshape(b, SB_HEADS, SB_HEAD_DIM, -1).transpose(0, 3, 1, 2)
            ks.append(heads_last(k))
            vs.append(heads_last(v))
            pools.append(tail[:, 1:])
        else:
            x, s = _odd_prompt(x, mod(1), gp(1), gq(1), w_in_odd[li], w_out_odd[li], tm=512)
            rets.append(s)
        x = _ffn_sublayer(x, mod(2), gp(2), gq(2), ffn_w_in[l][1], ffn_w_out[l][1], tm=512)
    return x, jnp.stack(ks, 0), jnp.stack(vs, 0), jnp.stack(pools, 0), jnp.stack(rets, 0)


def kernel(x_prompt, x_sample, c_prompt, c_sample, cache_k, cache_v, state_pool, state_ret, page_table, w_ada, b_ada, norm_pre, norm_post, ffn_w_in, ffn_w_out, w_in_even, sb_bias, pool_w, pool_scale, w_out_even, w_in_odd, w_out_odd):
    modp, mods = _modulation(c_prompt, c_sample, w_ada, b_ada)
    per_sublayer = lambda w: tuple(tuple(w[l, s].astype(BF16) for s in range(w.shape[1])) for l in range(w.shape[0]))
    weights = (per_sublayer(ffn_w_in), per_sublayer(ffn_w_out), w_in_even.astype(BF16))
    weights += (sb_bias, pool_w.astype(BF16), pool_scale) + tuple(w.astype(BF16) for w in (w_out_even, w_in_odd, w_out_odd))
    y_p, kp, vp, pp, rp = _prompt_trunk(x_prompt, modp, norm_pre, norm_post, *weights)
    y_s, ks, vs, ps, rs = _sample_trunk(x_sample, mods, cache_k, cache_v, state_pool, state_ret, page_table,
                                        norm_pre, norm_post, *weights)
    return (y_p, y_s, kp, vp, pp, rp, ks, vs, ps, rs)
```

```python
import functools

import numpy as np
import jax
import jax.numpy as jnp
from jax import lax
from jax.experimental import pallas as pl
from jax.experimental.pallas import tpu as pltpu

F32 = jnp.float32
BF16 = jnp.bfloat16

D_MODEL = 1024
D_FF = 2816
N_SUB = 3
NORM_EPS = 1e-6
SB_HEADS = 8
SB_HEAD_DIM = 64
SB_WIDTH = SB_HEADS * SB_HEAD_DIM
N_PAIRS = SB_HEADS // 2
POOL_WINDOWS = (2, 4, 8, 16)
POOL_GROUP = 128
POOL_WIDTH = len(POOL_WINDOWS) * POOL_GROUP
POOL_HIST = 16
RET_HEADS = 4
RET_QK_DIM = 256
RET_V_DIM = 512
RET_QK_WIDTH = RET_HEADS * RET_QK_DIM
RET_V_WIDTH = RET_HEADS * RET_V_DIM
RET_CHUNK = 128
ROPE_BASE = 10000.0
LN_EPS = 1e-5
LANES = 128
KBLK = 256
SUFFIX_PAD = 16
LOG2E = 1.4426950408889634
MASKED_LOGIT = -1e30
VMEM_LIMIT_BYTES = 56 * 1024 * 1024


def _params(*sem):
    return pltpu.CompilerParams(dimension_semantics=sem, vmem_limit_bytes=VMEM_LIMIT_BYTES)


def _resident(shape):
    nd = len(shape)
    return pl.BlockSpec(shape, lambda *_: (0,) * nd, pipeline_mode=pl.Buffered(1))


def _rms(x, g):
    ms = jnp.mean(x * x, axis=-1, keepdims=True)
    return x * lax.rsqrt(ms + NORM_EPS) * g


def _adaln_pre(x, m_ref, g):
    return _rms(x, g) * (1.0 + m_ref[0, 1]) + m_ref[0, 0]


def _silu(x):
    return x * jax.nn.sigmoid(x)


def _dot(a, b):
    return jnp.dot(a, b, preferred_element_type=F32)


def _dot_nt(a, b):
    return lax.dot_general(a, b, (((1,), (1,)), ((), ())), preferred_element_type=F32)


def _dot_tn(a, b):
    return lax.dot_general(a, b, (((0,), (0,)), ((), ())), preferred_element_type=F32)


def _mod_kernel(cp_ref, cs_ref, w_ref, b_ref, mp_ref, ms_ref):
    w = w_ref[0].astype(BF16)
    b = b_ref[0, 0]
    mp_ref[0, 0] = _dot(_silu(cp_ref[...]).astype(BF16), w) + b
    ms_ref[0, 0] = _dot(_silu(cs_ref[...]).astype(BF16), w) + b


def _modulation(c_p, c_s, w_ada, b_ada):
    depth = w_ada.shape[0]
    nb = N_SUB * 3
    bp, bs = c_p.shape[0], c_s.shape[0]
    return pl.pallas_call(
        _mod_kernel,
        grid=(depth, nb),
        in_specs=[
            pl.BlockSpec((bp, D_MODEL), lambda l, n: (0, 0)),
            pl.BlockSpec((bs, D_MODEL), lambda l, n: (0, 0)),
            pl.BlockSpec((1, D_MODEL, D_MODEL), lambda l, n: (l, 0, n)),
            pl.BlockSpec((1, 1, 1, D_MODEL), lambda l, n: (l, n, 0, 0)),
        ],
        out_specs=[
            pl.BlockSpec((1, 1, bp, D_MODEL), lambda l, n: (l, n, 0, 0)),
            pl.BlockSpec((1, 1, bs, D_MODEL), lambda l, n: (l, n, 0, 0)),
        ],
        out_shape=[
            jax.ShapeDtypeStruct((depth, nb, bp, D_MODEL), F32),
            jax.ShapeDtypeStruct((depth, nb, bs, D_MODEL), F32),
        ],
        compiler_params=_params("arbitrary", "arbitrary"),
        name="adaln_modulation",
    )(c_p, c_s, w_ada, b_ada.reshape(depth, nb, 1, D_MODEL))


def _ffn_kernel(x_ref, m_ref, gpre_ref, gpost_ref, win_ref, wout_ref, o_ref):
    x = x_ref[0]
    hb = _adaln_pre(x, m_ref, gpre_ref[...]).astype(BF16)
    gu = _dot(hb, win_ref[...])
    a = (_silu(gu[:, :D_FF]) * gu[:, D_FF:]).astype(BF16)
    y = _dot(a, wout_ref[...])
    o_ref[0] = x + 0.5 * m_ref[0, 2] * _rms(y, gpost_ref[...])


def _resident_member(stack, index):
    lead = len(index)
    shape = (None,) * lead + tuple(stack.shape[lead:])
    tail = (0,) * (stack.ndim - lead)
    return pl.BlockSpec(shape, lambda *_: tuple(index) + tail, pipeline_mode=pl.Buffered(1))


def _ffn_sublayer(x, mod, g_pre, g_post, w_in, w_out, member, tm):
    g, t, _ = x.shape
    r = mod.shape[2]
    rb = 1 if r == 1 else tm
    return pl.pallas_call(
        _ffn_kernel,
        grid=(g, t // tm),
        in_specs=[
            pl.BlockSpec((1, tm, D_MODEL), lambda b, j: (b, j, 0)),
            pl.BlockSpec((1, 3, rb, D_MODEL), (lambda b, j: (b, 0, 0, 0)) if r == 1 else (lambda b, j: (b, 0, j, 0))),
            _resident((1, D_MODEL)),
            _resident((1, D_MODEL)),
            _resident_member(w_in, member),
            _resident_member(w_out, member),
        ],
        out_specs=pl.BlockSpec((1, tm, D_MODEL), lambda b, j: (b, j, 0)),
        out_shape=jax.ShapeDtypeStruct(x.shape, F32),
        compiler_params=_params("arbitrary", "arbitrary"),
        name="ffn_sublayer",
    )(x, mod, g_pre, g_post, w_in, w_out)


def _softplus(z):
    return jnp.maximum(z, 0.0) + jnp.log(1.0 + jnp.exp(-jnp.abs(z)))


def _even_prompt_kernel(bias_ref, x_ref, m_ref, gpre_ref, gpost_ref, win_ref, pw_ref, ps_ref, wout_ref,
                        xo_ref, k_ref, v_ref, tail_ref,
                        kbuf, vtbuf, ubuf, obuf, acc_ref, car_ref, *, tm, tq):
    j = pl.program_id(1)
    nj = pl.num_programs(1)
    x = x_ref[0]
    hb = _adaln_pre(x, m_ref, gpre_ref[...]).astype(BF16)
    qk = _dot(hb, win_ref[:, 0:2 * SB_WIDTH])
    q = qk[:, 0:SB_WIDTH] * (SB_HEAD_DIM ** -0.5)
    k = qk[:, SB_WIDTH:]
    nsub = tm // KBLK
    for s in range(nsub):
        kbuf[j * nsub + s] = k[s * KBLK:(s + 1) * KBLK].astype(BF16)

    @pl.when(j == 0)
    def _():
        ubuf[0:POOL_HIST] = jnp.zeros((POOL_HIST, POOL_WIDTH), F32)

    def project_values():
        k_ref[0] = k.T
        vu = _dot(hb, win_ref[:, 2 * SB_WIDTH:])
        v_t = vu[:, 0:SB_WIDTH].T
        v_ref[0] = v_t
        for s in range(nsub):
            vtbuf[j * nsub + s] = v_t[:, s * KBLK:(s + 1) * KBLK].astype(BF16)
        ubuf[POOL_HIST:POOL_HIST + tm] = vu[:, SB_WIDTH:]

    lane = lax.broadcasted_iota(jnp.int32, (tq, 2 * SB_HEAD_DIM), 1)
    krow = lax.broadcasted_iota(jnp.int32, (KBLK, 2 * tq), 0)
    qcol = lax.broadcasted_iota(jnp.int32, (KBLK, 2 * tq), 1)
    qcol = jnp.where(qcol >= tq, qcol - tq, qcol)
    first_head = lax.broadcasted_iota(jnp.int32, (1, 2 * tq), 1) < tq
    mr = lax.broadcasted_iota(jnp.int32, (KBLK + SUFFIX_PAD, 2 * KBLK), 0)
    mc = lax.broadcasted_iota(jnp.int32, (KBLK + SUFFIX_PAD, 2 * KBLK), 1)
    mc = jnp.where(mc >= KBLK, mc - KBLK, mc)
    suffix_neg = jnp.where((mc > mr) | (mr == KBLK), -1.0, 0.0).astype(BF16)

    for qt in range(tm // tq):
        qtile = j * (tm // tq) + qt
        qrows = q[qt * tq:(qt + 1) * tq]
        qs_t, bias_row = [], []
        for p in range(N_PAIRS):
            q2 = qrows[:, 128 * p:128 * (p + 1)]
            qa = jnp.where(lane < SB_HEAD_DIM, q2, 0.0).T
            qb = jnp.where(lane >= SB_HEAD_DIM, q2, 0.0).T
            qs_t.append(jnp.concatenate([qa, qb], axis=1).astype(BF16))
            bias_row.append(jnp.where(first_head, bias_ref[2 * p], bias_ref[2 * p + 1]) * LOG2E)
        acc_ref[...] = jnp.zeros_like(acc_ref)
        car_ref[...] = jnp.zeros_like(car_ref)

        def step(kb, masked, after_scores=None):
            kblk = kbuf[kb]
            zs = [_dot(kblk[:, 128 * p:128 * (p + 1)], qs_t[p]) for p in range(N_PAIRS)]
            if after_scores is not None:
                after_scores()
            if masked:
                hide = jnp.where((kb * KBLK + krow) < (qtile * tq + qcol), 0.0, MASKED_LOGIT)
            logbs, hls = [], []
            for p in range(N_PAIRS):
                z = zs[p] * LOG2E + bias_row[p]
                if masked:
                    z = z + hide
                sp = jnp.maximum(z, 0.0) + jnp.log2(1.0 + jnp.exp2(-jnp.abs(z)))
                logbs.append(z - sp)
                hi = sp.astype(BF16)
                lo = (sp - hi.astype(F32)).astype(BF16)
                hls.append(jnp.concatenate([hi, lo], axis=0))
            sums = [_dot(suffix_neg, hls[p]) for p in range(N_PAIRS)]
            ws = []
            for p in range(N_PAIRS):
                ws.append(jnp.exp2(logbs[p] + sums[p][0:KBLK] + car_ref[p]).astype(BF16))
                car_ref[p] += sums[p][KBLK:KBLK + 1]
            vtb = vtbuf[kb]
            for p in range(N_PAIRS):
                acc_ref[p] += _dot(vtb[128 * p:128 * (p + 1), :], ws[p])

        ndiag = tq // KBLK
        n_full = qtile * ndiag
        for dk in range(ndiag - 1, -1, -1):
            step(n_full + dk, True, project_values if dk == ndiag - 1 else None)

        def body(it, carry):
            step(n_full - 1 - it, False)
            return carry

        lax.fori_loop(0, n_full, body, 0)

        for p in range(N_PAIRS):
            a = acc_ref[p]
            o_t = jnp.concatenate([a[0:SB_HEAD_DIM, 0:tq], a[SB_HEAD_DIM:, tq:]], axis=0)
            obuf[qt * tq:(qt + 1) * tq, 128 * p:128 * (p + 1)] = o_t.T.astype(BF16)

    y = _dot(obuf[:, 0:SB_WIDTH], wout_ref[0:SB_WIDTH, :])

    pos1 = (j * tm + 1 + lax.broadcasted_iota(jnp.int32, (tm, POOL_GROUP), 0)).astype(F32)
    for gi, wdw in enumerate(POOL_WINDOWS):
        lo_, hi_ = gi * POOL_GROUP, (gi + 1) * POOL_GROUP
        ug = ubuf[POOL_HIST:POOL_HIST + tm, lo_:hi_]
        ssum = ug
        for sft in range(1, wdw):
            ssum = ssum + ubuf[POOL_HIST - sft:POOL_HIST - sft + tm, lo_:hi_]
        d = ssum / jnp.minimum(pos1, float(wdw)) - ug
        yg = _dot(d.astype(BF16), pw_ref[gi]) * ps_ref[:, lo_:hi_]
        obuf[:, SB_WIDTH + lo_:SB_WIDTH + hi_] = yg.astype(BF16)
    hist = ubuf[tm:tm + POOL_HIST]
    ubuf[0:POOL_HIST] = hist

    @pl.when(j == nj - 1)
    def _():
        tail_ref[0] = hist

    y = y + _dot(obuf[:, SB_WIDTH:], wout_ref[SB_WIDTH:, :])
    xo_ref[0] = x + m_ref[0, 2] * _rms(y, gpost_ref[...])


def _even_prompt(x, mod, g_pre, g_post, w_in, sb_bias, pool_w, pool_scale, w_out, tm, tq):
    assert tm == tq and tq % KBLK == 0
    b, t, _ = x.shape
    nkb = t // KBLK
    grid_spec = pltpu.PrefetchScalarGridSpec(
        num_scalar_prefetch=1,
        grid=(b, t // tm),
        in_specs=[
            pl.BlockSpec((1, tm, D_MODEL), lambda i, j, *_: (i, j, 0)),
            pl.BlockSpec((1, 3, 1, D_MODEL), lambda i, j, *_: (i, 0, 0, 0)),
            _resident((1, D_MODEL)),
            _resident((1, D_MODEL)),
            _resident(w_in.shape),
            _resident(pool_w.shape),
            _resident((1, POOL_WIDTH)),
            _resident(w_out.shape),
        ],
        out_specs=[
            pl.BlockSpec((1, tm, D_MODEL), lambda i, j, *_: (i, j, 0)),
            pl.BlockSpec((1, SB_WIDTH, tm), lambda i, j, *_: (i, 0, j)),
            pl.BlockSpec((1, SB_WIDTH, tm), lambda i, j, *_: (i, 0, j)),
            pl.BlockSpec((1, POOL_HIST, POOL_WIDTH), lambda i, j, *_: (i, 0, 0)),
        ],
        scratch_shapes=[
            pltpu.VMEM((nkb, KBLK, SB_WIDTH), BF16),
            pltpu.VMEM((nkb, SB_WIDTH, KBLK), BF16),
            pltpu.VMEM((POOL_HIST + tm, POOL_WIDTH), F32),
            pltpu.VMEM((tm, SB_WIDTH + POOL_WIDTH), BF16),
            pltpu.VMEM((N_PAIRS, 2 * SB_HEAD_DIM, 2 * tq), F32),
            pltpu.VMEM((N_PAIRS, 1, 2 * tq), F32),
        ],
    )
    return pl.pallas_call(
        functools.partial(_even_prompt_kernel, tm=tm, tq=tq),
        grid_spec=grid_spec,
        out_shape=[
            jax.ShapeDtypeStruct(x.shape, F32),
            jax.ShapeDtypeStruct((b, SB_WIDTH, t), F32),
            jax.ShapeDtypeStruct((b, SB_WIDTH, t), F32),
            jax.ShapeDtypeStruct((b, POOL_HIST, POOL_WIDTH), F32),
        ],
        compiler_params=_params("arbitrary", "arbitrary"),
        name="even_prompt",
    )(sb_bias, x, mod, g_pre, g_post, w_in, pool_w, pool_scale, w_out)


def _ret_log_gamma():
    return np.log(np.float32(1.0) - np.exp2(np.float32(-5.0) - np.arange(RET_HEADS, dtype=np.float32))).astype(np.float32)


def _ret_tables(chunk):
    lg = jnp.asarray(_ret_log_gamma())
    i = jnp.arange(chunk, dtype=F32)
    diff = i[:, None] - i[None, :]
    decay = jnp.where(diff[None] >= 0, jnp.exp(jnp.maximum(diff, 0.0)[None] * lg[:, None, None]), 0.0)
    cross = jnp.exp((i[None, :] + 1.0) * lg[:, None])[:, :, None]
    kdec = jnp.exp((chunk - 1.0 - i)[None, :] * lg[:, None])[:, :, None]
    full = jnp.exp(chunk * lg)
    return decay, cross, kdec, full


def _rope_tables(pos):
    half = RET_QK_DIM // 2
    inv = ROPE_BASE ** (-jnp.arange(half, dtype=F32) / half)
    ang = pos[:, None] * inv[None, :]
    return jnp.cos(ang), jnp.sin(ang)


def _rotate(x, cos, sin):
    half = RET_QK_DIM // 2
    x1, x2 = x[:, :half], x[:, half:]
    return jnp.concatenate([x1 * cos - x2 * sin, x1 * sin + x2 * cos], axis=-1)


def _group_norm_gate(o, g):
    mu = jnp.mean(o, axis=-1, keepdims=True)
    c = o - mu
    var = jnp.mean(c * c, axis=-1, keepdims=True)
    return c * lax.rsqrt(var + LN_EPS) * _silu(g)


def _odd_prompt_kernel(full_ref, x_ref, m_ref, gpre_ref, gpost_ref, cos_ref, sin_ref, win_ref, wout_ref,
                       dec_ref, crs_ref, kdc_ref, xo_ref, s_ref, obuf, *, tm):
    j = pl.program_id(1)

    @pl.when(j == 0)
    def _():
        s_ref[...] = jnp.zeros_like(s_ref)

    x = x_ref[0]
    hb = _adaln_pre(x, m_ref, gpre_ref[...]).astype(BF16)
    cos, sin = cos_ref[...], sin_ref[...]
    qrs, krs, vhs, ghs = [], [], [], []
    for hd in range(RET_HEADS):
        qo = hd * RET_QK_DIM
        ko = RET_QK_WIDTH + hd * RET_QK_DIM
        vo = 2 * RET_QK_WIDTH + hd * RET_V_DIM
        go = 2 * RET_QK_WIDTH + RET_V_WIDTH + hd * RET_V_DIM
        qrs.append(_rotate(_dot(hb, win_ref[:, qo:qo + RET_QK_DIM]), cos, sin).astype(BF16))
        krs.append(_rotate(_dot(hb, win_ref[:, ko:ko + RET_QK_DIM]), cos, sin) * (RET_QK_DIM ** -0.5))
        vhs.append(_dot(hb, win_ref[:, vo:vo + RET_V_DIM]).astype(BF16))
        ghs.append(_dot(hb, win_ref[:, go:go + RET_V_DIM]))
    for c in range(tm // RET_CHUNK):
        rows = slice(c * RET_CHUNK, (c + 1) * RET_CHUNK)
        for hd in range(RET_HEADS):
            qc = qrs[hd][rows]
            kc = krs[hd][rows]
            vc = vhs[hd][rows]
            scores = _dot_nt(qc, kc.astype(BF16)) * dec_ref[hd]
            state = s_ref[0, hd]
            o = _dot(scores.astype(BF16), vc) + _dot(qc, state.astype(BF16)) * crs_ref[hd]
            kd = (kc * kdc_ref[hd]).astype(BF16)
            s_ref[0, hd] = full_ref[hd] * state + _dot_tn(kd, vc)
            obuf[rows, hd * RET_V_DIM:(hd + 1) * RET_V_DIM] = _group_norm_gate(o, ghs[hd][rows]).astype(BF16)
    y = _dot(obuf[...], wout_ref[...])
    xo_ref[0] = x + m_ref[0, 2] * _rms(y, gpost_ref[...])


def _odd_prompt(x, mod, g_pre, g_post, w_in, w_out, tm):
    b, t, _ = x.shape
    cos, sin = _rope_tables(jnp.arange(t, dtype=F32))
    decay, cross, kdec, full = _ret_tables(RET_CHUNK)
    half = RET_QK_DIM // 2
    grid_spec = pltpu.PrefetchScalarGridSpec(
        num_scalar_prefetch=1,
        grid=(b, t // tm),
        in_specs=[
            pl.BlockSpec((1, tm, D_MODEL), lambda i, j, *_: (i, j, 0)),
            pl.BlockSpec((1, 3, 1, D_MODEL), lambda i, j, *_: (i, 0, 0, 0)),
            _resident((1, D_MODEL)),
            _resident((1, D_MODEL)),
            pl.BlockSpec((tm, half), lambda i, j, *_: (j, 0)),
            pl.BlockSpec((tm, half), lambda i, j, *_: (j, 0)),
            _resident(w_in.shape),
            _resident(w_out.shape),
            _resident(decay.shape),
            _resident(cross.shape),
            _resident(kdec.shape),
        ],
        out_specs=[
            pl.BlockSpec((1, tm, D_MODEL), lambda i, j, *_: (i, j, 0)),
            pl.BlockSpec((1, RET_HEADS, RET_QK_DIM, RET_V_DIM), lambda i, j, *_: (i, 0, 0, 0)),
        ],
        scratch_shapes=[pltpu.VMEM((tm, RET_V_WIDTH), BF16)],
    )
    return pl.pallas_call(
        functools.partial(_odd_prompt_kernel, tm=tm),
        grid_spec=grid_spec,
        out_shape=[
            jax.ShapeDtypeStruct(x.shape, F32),
            jax.ShapeDtypeStruct((b, RET_HEADS, RET_QK_DIM, RET_V_DIM), F32),
        ],
        compiler_params=_params("arbitrary", "arbitrary"),
        name="odd_prompt",
    )(full, x, mod, g_pre, g_post, cos, sin, w_in, w_out, decay, cross, kdec)


def _even_sample_pre_kernel(x_ref, m_ref, gpre_ref, win_ref, pool_ref, pw_ref, ps_ref,
                            q_ref, k_ref, v_ref, pool_out_ref, opool_ref):
    hb = _adaln_pre(x_ref[...], m_ref, gpre_ref[...]).astype(BF16)
    qkvu = _dot(hb, win_ref[...])
    q_ref[...] = qkvu[:, 0:SB_WIDTH] * (SB_HEAD_DIM ** -0.5)
    k_ref[...] = qkvu[:, SB_WIDTH:2 * SB_WIDTH].T
    v_ref[...] = qkvu[:, 2 * SB_WIDTH:3 * SB_WIDTH].T
    u = qkvu[:, 3 * SB_WIDTH:]
    nh = POOL_HIST - 1
    for gi, wdw in enumerate(POOL_WINDOWS):
        lo_, hi_ = gi * POOL_GROUP, (gi + 1) * POOL_GROUP
        ug = u[:, lo_:hi_]
        ssum = ug
        for back in range(1, wdw):
            ssum = ssum + pool_ref[nh - back, :, lo_:hi_]
        d = ssum / float(wdw) - ug
        opool_ref[:, lo_:hi_] = (_dot(d.astype(BF16), pw_ref[gi]) * ps_ref[:, lo_:hi_]).astype(BF16)
    pool_out_ref[0:nh - 1] = pool_ref[1:nh]
    pool_out_ref[nh - 1] = u


def _even_sample_pre(x, mod, g_pre, w_in, pool_hist, pool_w, pool_scale):
    rows = x.shape[0]
    outs = [
        jax.ShapeDtypeStruct((rows, SB_WIDTH), F32),
        jax.ShapeDtypeStruct((SB_WIDTH, rows), F32),
        jax.ShapeDtypeStruct((SB_WIDTH, rows), F32),
        jax.ShapeDtypeStruct(pool_hist.shape, F32),
        jax.ShapeDtypeStruct((rows, POOL_WIDTH), BF16),
    ]
    return pl.pallas_call(
        _even_sample_pre_kernel,
        out_shape=outs,
        compiler_params=pltpu.CompilerParams(vmem_limit_bytes=VMEM_LIMIT_BYTES),
        name="even_sample_pre",
    )(x, mod, g_pre, w_in, pool_hist, pool_w, pool_scale)


def _sb_decode_kernel(pt_ref, bias_ref, q_ref, ck_ref, cv_ref, o_ref, kbuf, vbuf, sem, *, n_pages):
    i = pl.program_id(0)
    n = pl.num_programs(0)

    def copies(sample, slot):
        out = []
        for p in range(n_pages):
            pg = pt_ref[sample, p]
            out.append(pltpu.make_async_copy(ck_ref.at[pg], kbuf.at[slot, p], sem.at[0, slot]))
            out.append(pltpu.make_async_copy(cv_ref.at[pg], vbuf.at[slot, p], sem.at[1, slot]))
        return out

    @pl.when(i == 0)
    def _():
        for c in copies(0, 0):
            c.start()

    @pl.when(i + 1 < n)
    def _():
        for c in copies(i + 1, (i + 1) % 2):
            c.start()

    slot = i % 2
    for c in copies(i, slot):
        c.wait()

    q = q_ref[0]
    head_of_lane = lax.broadcasted_iota(jnp.int32, (SB_HEADS, SB_WIDTH), 1) // SB_HEAD_DIM
    own = head_of_lane == lax.broadcasted_iota(jnp.int32, (SB_HEADS, SB_WIDTH), 0)
    qblk = jnp.where(own, q, 0.0).astype(BF16)
    hrow = lax.broadcasted_iota(jnp.int32, (SB_HEADS, 1), 0)
    bias = jnp.zeros((SB_HEADS, 1), F32)
    for h in range(SB_HEADS):
        bias = jnp.where(hrow == h, bias_ref[h], bias)
    sps, logbs = [], []
    for p in range(n_pages):
        z = _dot(qblk, kbuf[slot, p].astype(BF16)) + bias
        sp = _softplus(z)
        sps.append(sp)
        logbs.append(z - sp)
    page = sps[0].shape[1]
    st = jnp.concatenate(sps, axis=0)
    hi = st.astype(BF16)
    lo = (st - hi.astype(F32)).astype(BF16)
    hl = jnp.concatenate([hi, lo], axis=1)
    jr = lax.broadcasted_iota(jnp.int32, (2 * page, page), 0)
    jr = jnp.where(jr >= page, jr - page, jr)
    sc = lax.broadcasted_iota(jnp.int32, (2 * page, page), 1)
    within = _dot(hl, jnp.where(jr > sc, -1.0, 0.0).astype(BF16))
    total = _dot(hl, jnp.full((2 * page, page), -1.0, BF16))
    carry = jnp.zeros((SB_HEADS, page), F32)
    o8 = jnp.zeros((SB_HEADS, SB_WIDTH), F32)
    for p in range(n_pages - 1, -1, -1):
        rows = slice(p * SB_HEADS, (p + 1) * SB_HEADS)
        w = jnp.exp(logbs[p] + within[rows] + carry).astype(BF16)
        carry = carry + total[rows]
        o8 = o8 + _dot_nt(w, vbuf[slot, p].astype(BF16))
    o_ref[0] = jnp.sum(jnp.where(own, o8, 0.0), axis=0, keepdims=True)


def _sb_decode(q, sb_bias, page_table, cache_k, cache_v):
    rows, n_pages = page_table.shape
    page = cache_k.shape[2]
    grid_spec = pltpu.PrefetchScalarGridSpec(
        num_scalar_prefetch=2,
        grid=(rows,),
        in_specs=[
            pl.BlockSpec((1, 1, SB_WIDTH), lambda i, *_: (i, 0, 0)),
            pl.BlockSpec(memory_space=pl.ANY),
            pl.BlockSpec(memory_space=pl.ANY),
        ],
        out_specs=pl.BlockSpec((1, 1, SB_WIDTH), lambda i, *_: (i, 0, 0)),
        scratch_shapes=[
            pltpu.VMEM((2, n_pages, SB_WIDTH, page), F32),
            pltpu.VMEM((2, n_pages, SB_WIDTH, page), F32),
            pltpu.SemaphoreType.DMA((2, 2)),
        ],
    )
    out = pl.pallas_call(
        functools.partial(_sb_decode_kernel, n_pages=n_pages),
        grid_spec=grid_spec,
        out_shape=jax.ShapeDtypeStruct((rows, 1, SB_WIDTH), F32),
        compiler_params=_params("arbitrary"),
        name="sb_decode",
    )(page_table, sb_bias, q.reshape(rows, 1, SB_WIDTH), cache_k, cache_v)
    return out.reshape(rows, SB_WIDTH)


def _even_sample_post_kernel(x_ref, m_ref, gpost_ref, osb_ref, opool_ref, wout_ref, xo_ref):
    y = _dot(osb_ref[...].astype(BF16), wout_ref[0:SB_WIDTH, :]) + _dot(opool_ref[...], wout_ref[SB_WIDTH:, :])
    xo_ref[...] = x_ref[...] + m_ref[0, 2] * _rms(y, gpost_ref[...])


def _even_sample_post(x, mod, g_post, o_sb, o_pool, w_out):
    return pl.pallas_call(
        _even_sample_post_kernel,
        out_shape=jax.ShapeDtypeStruct(x.shape, F32),
        compiler_params=pltpu.CompilerParams(vmem_limit_bytes=VMEM_LIMIT_BYTES),
        name="even_sample_post",
    )(x, mod, g_post, o_sb, o_pool, w_out)


def _odd_sample_pre_kernel(x_ref, m_ref, gpre_ref, cos_ref, sin_ref, win_ref, q_ref, k_ref, v_ref, g_ref):
    hb = _adaln_pre(x_ref[...], m_ref, gpre_ref[...]).astype(BF16)
    cos, sin = cos_ref[...], sin_ref[...]
    for hd in range(RET_HEADS):
        qo = hd * RET_QK_DIM
        ko = RET_QK_WIDTH + hd * RET_QK_DIM
        q_ref[:, qo:qo + RET_QK_DIM] = _rotate(_dot(hb, win_ref[:, qo:qo + RET_QK_DIM]), cos, sin)
        k_ref[:, qo:qo + RET_QK_DIM] = _rotate(_dot(hb, win_ref[:, ko:ko + RET_QK_DIM]), cos, sin) * (RET_QK_DIM ** -0.5)
    vo = 2 * RET_QK_WIDTH
    v_ref[...] = _dot(hb, win_ref[:, vo:vo + RET_V_WIDTH])
    g_ref[...] = _dot(hb, win_ref[:, vo + RET_V_WIDTH:])


def _odd_sample_pre(x, mod, g_pre, cos, sin, w_in):
    rows = x.shape[0]
    outs = [
        jax.ShapeDtypeStruct((rows, RET_QK_WIDTH), F32),
        jax.ShapeDtypeStruct((rows, RET_QK_WIDTH), F32),
        jax.ShapeDtypeStruct((rows, RET_V_WIDTH), F32),
        jax.ShapeDtypeStruct((rows, RET_V_WIDTH), F32),
    ]
    return pl.pallas_call(
        _odd_sample_pre_kernel,
        out_shape=outs,
        compiler_params=pltpu.CompilerParams(vmem_limit_bytes=VMEM_LIMIT_BYTES),
        name="odd_sample_pre",
    )(x, mod, g_pre, cos, sin, w_in)


def _retention_token(gam_ref, q, k, v, s_ref, so_ref):
    row0 = lax.broadcasted_iota(jnp.int32, (16, RET_QK_DIM), 0) == 0
    outs = []
    for hd in range(RET_HEADS):
        qh = q[:, hd * RET_QK_DIM:(hd + 1) * RET_QK_DIM]
        kh = k[:, hd * RET_QK_DIM:(hd + 1) * RET_QK_DIM]
        vh = v[:, hd * RET_V_DIM:(hd + 1) * RET_V_DIM]
        qb = qh.astype(BF16)
        vb = vh.astype(BF16).astype(F32)
        score = jnp.sum(qb.astype(F32) * kh.astype(BF16).astype(F32), axis=-1, keepdims=True)
        state = s_ref[hd]
        q16 = jnp.where(row0, jnp.broadcast_to(qh, (16, RET_QK_DIM)), 0.0).astype(BF16)
        cross = _dot(q16, state.astype(BF16))[0:1] * gam_ref[hd]
        outs.append(score.astype(BF16).astype(F32) * vb + cross)
        kcol = jnp.broadcast_to(kh, (8, RET_QK_DIM)).T[:, 0:1]
        so_ref[hd] = gam_ref[hd] * state + kcol * vh
    return jnp.concatenate(outs, axis=-1)


def _ffn_ret_kernel(gam_ref, x_ref, m_ref, gpre_ref, gpost_ref, win_ref, wout_ref, q_ref, k_ref, v_ref, s_hbm,
                    xo_ref, o_ref, so_hbm, s_in, s_out, sem_in, sem_out, *, nb):
    t = pl.program_id(0) * pl.num_programs(1) + pl.program_id(1)
    n = pl.num_programs(0) * pl.num_programs(1)
    slot = t % 2

    def load(step, sl):
        return pltpu.make_async_copy(s_hbm.at[pl.ds(step * nb, nb)], s_in.at[sl], sem_in.at[sl])

    def store(step, sl):
        return pltpu.make_async_copy(s_out.at[sl], so_hbm.at[pl.ds(step * nb, nb)], sem_out.at[sl])

    @pl.when(t == 0)
    def _():
        load(0, 0).start()

    @pl.when(t + 1 < n)
    def _():
        load(t + 1, 1 - slot).start()

    load(t, slot).wait()

    @pl.when(t >= 2)
    def _():
        store(t - 2, slot).wait()

    for i in range(nb):
        b = t * nb + i
        o_ref[pl.ds(b, 1), :] = _retention_token(gam_ref, q_ref[pl.ds(b, 1), :], k_ref[pl.ds(b, 1), :],
                                                 v_ref[pl.ds(b, 1), :], s_in.at[slot, i], s_out.at[slot, i])
    store(t, slot).start()

    _ffn_kernel(x_ref, m_ref, gpre_ref, gpost_ref, win_ref, wout_ref, xo_ref)

    @pl.when(t == n - 1)
    def _():
        @pl.when(n >= 2)
        def _():
            store(t - 1, 1 - slot).wait()
        store(t, slot).wait()


def _ffn_sublayer_with_retention_step(x, mod, g_pre, g_post, w_in, w_out, member, q, k, v, state, tm):
    g, t, _ = x.shape
    rows = q.shape[0]
    steps = g * (t // tm)
    nb = rows // steps
    assert nb * steps == rows and mod.shape[2] == 1
    gamma = jnp.exp(jnp.asarray(_ret_log_gamma()))
    state_buf = pltpu.VMEM((2, nb, RET_HEADS, RET_QK_DIM, RET_V_DIM), F32)
    grid_spec = pltpu.PrefetchScalarGridSpec(
        num_scalar_prefetch=1,
        grid=(g, t // tm),
        in_specs=[
            pl.BlockSpec((1, tm, D_MODEL), lambda b, j, *_: (b, j, 0)),
            pl.BlockSpec((1, 3, 1, D_MODEL), lambda b, j, *_: (b, 0, 0, 0)),
            _resident((1, D_MODEL)),
            _resident((1, D_MODEL)),
            _resident_member(w_in, member),
            _resident_member(w_out, member),
            _resident(q.shape),
            _resident(k.shape),
            _resident(v.shape),
            pl.BlockSpec(memory_space=pl.ANY),
        ],
        out_specs=[
            pl.BlockSpec((1, tm, D_MODEL), lambda b, j, *_: (b, j, 0)),
            pl.BlockSpec((rows, RET_V_WIDTH), lambda b, j, *_: (0, 0)),
            pl.BlockSpec(memory_space=pl.ANY),
        ],
        scratch_shapes=[state_buf, state_buf, pltpu.SemaphoreType.DMA((2,)), pltpu.SemaphoreType.DMA((2,))],
    )
    return pl.pallas_call(
        functools.partial(_ffn_ret_kernel, nb=nb),
        grid_spec=grid_spec,
        out_shape=[
            jax.ShapeDtypeStruct(x.shape, F32),
            jax.ShapeDtypeStruct((rows, RET_V_WIDTH), F32),
            jax.ShapeDtypeStruct(state.shape, F32),
        ],
        compiler_params=_params("arbitrary", "arbitrary"),
        name="ffn_sublayer_retention_step",
    )(gamma, x, mod, g_pre, g_post, w_in, w_out, q, k, v, state)


def _odd_sample_post_kernel(x_ref, m_ref, gpost_ref, o_ref, g_ref, wout_ref, xo_ref, obuf):
    for hd in range(RET_HEADS):
        cols = slice(hd * RET_V_DIM, (hd + 1) * RET_V_DIM)
        obuf[:, cols] = _group_norm_gate(o_ref[:, cols], g_ref[:, cols]).astype(BF16)
    y = _dot(obuf[...], wout_ref[...])
    xo_ref[...] = x_ref[...] + m_ref[0, 2] * _rms(y, gpost_ref[...])


def _odd_sample_post(x, mod, g_post, o, g, w_out):
    return pl.pallas_call(
        _odd_sample_post_kernel,
        out_shape=jax.ShapeDtypeStruct(x.shape, F32),
        scratch_shapes=[pltpu.VMEM((x.shape[0], RET_V_WIDTH), BF16)],
        compiler_params=pltpu.CompilerParams(vmem_limit_bytes=VMEM_LIMIT_BYTES),
        name="odd_sample_post",
    )(x, mod, g_post, o, g, w_out)


PROMPT_FFN_ROWS = 512
PROMPT_HOST_ROWS = 256
PROMPT_EVEN_ROWS = 256
PROMPT_ODD_ROWS = 512


def _trunks(xp, xs, modp, mods, cache_k, cache_v, state_pool, state_ret, page_table, norm_pre, norm_post, ffn_w_in,
            ffn_w_out, w_in_even, sb_bias, pool_w, pool_scale, w_out_even, w_in_odd, w_out_odd):
    depth = modp.shape[0]
    b = xp.shape[0]
    rows = xs.shape[0]
    page = cache_k.shape[2]
    past_len = page_table.shape[1] * page
    xs = xs.reshape(rows, D_MODEL)
    out_p = dict(k=[], v=[], pool=[], ret=[])
    out_s = dict(k=[], v=[], pool=[], ret=[])
    for l in range(depth):
        li = l // 2
        mod_p = lambda s: modp[l, 3 * s:3 * s + 3].transpose(1, 0, 2).reshape(b, 3, 1, D_MODEL)
        mod_s = lambda s: mods[l, 3 * s:3 * s + 3].reshape(1, 3, rows, D_MODEL)
        gp = lambda s: norm_pre[l, s].reshape(1, D_MODEL)
        gq = lambda s: norm_post[l, s].reshape(1, D_MODEL)
        ffn = lambda x, mod, s, half, tm: _ffn_sublayer(x, mod, gp(s), gq(s), ffn_w_in, ffn_w_out, (l, half), tm=tm)

        xp = ffn(xp, mod_p(0), 0, 0, PROMPT_FFN_ROWS)
        xs = ffn(xs[None], mod_s(0), 0, 0, rows)[0]
        if l % 2 == 0:
            xp, k, v, tail = _even_prompt(xp, mod_p(1), gp(1), gq(1), w_in_even[li], sb_bias[li], pool_w[li],
                                          pool_scale[li].reshape(1, POOL_WIDTH), w_out_even[li],
                                          tm=PROMPT_EVEN_ROWS, tq=PROMPT_EVEN_ROWS)
            heads_last = lambda a: a.reshape(b, SB_HEADS, SB_HEAD_DIM, -1).transpose(0, 3, 1, 2)
            out_p["k"].append(heads_last(k))
            out_p["v"].append(heads_last(v))
            out_p["pool"].append(tail[:, 1:])

            pool_hist = state_pool[li].transpose(1, 0, 2)
            q, k_t, v_t, pool_new, o_pool = _even_sample_pre(xs, mod_s(1), gp(1), w_in_even[li], pool_hist, pool_w[li],
                                                             pool_scale[li].reshape(1, POOL_WIDTH))
            pages_t = lambda c: c[li].transpose(0, 2, 3, 1).reshape(-1, SB_WIDTH, page)
            o_sb = _sb_decode(q, sb_bias[li], page_table, pages_t(cache_k), pages_t(cache_v))
            xs = _even_sample_post(xs, mod_s(1), gq(1), o_sb, o_pool, w_out_even[li])
            sample_heads_last = lambda a: a.reshape(SB_HEADS, SB_HEAD_DIM, rows).transpose(2, 0, 1)[:, None]
            out_s["k"].append(sample_heads_last(k_t))
            out_s["v"].append(sample_heads_last(v_t))
            out_s["pool"].append(pool_new.transpose(1, 0, 2))
            xp = ffn(xp, mod_p(2), 2, 1, PROMPT_FFN_ROWS)
        else:
            xp, s = _odd_prompt(xp, mod_p(1), gp(1), gq(1), w_in_odd[li], w_out_odd[li], tm=PROMPT_ODD_ROWS)
            out_p["ret"].append(s)
            cos, sin = _rope_tables(jnp.full((1,), past_len, F32))
            q, k, v, g = _odd_sample_pre(xs, mod_s(1), gp(1), cos, sin, w_in_odd[li])
            xp, o, s_new = _ffn_sublayer_with_retention_step(xp, mod_p(2), gp(2), gq(2), ffn_w_in, ffn_w_out, (l, 1),
                                                             q, k, v, state_ret[li], tm=PROMPT_HOST_ROWS)
            xs = _odd_sample_post(xs, mod_s(1), gq(1), o, g, w_out_odd[li])
            out_s["ret"].append(s_new)
        xs = ffn(xs[None], mod_s(2), 2, 1, rows)[0]
    stacked = lambda o: tuple(jnp.stack(o[n], 0) for n in ("k", "v", "pool", "ret"))
    return (xp,) + stacked(out_p), (xs.reshape(rows, 1, D_MODEL),) + stacked(out_s)


def kernel(x_prompt, x_sample, c_prompt, c_sample, cache_k, cache_v, state_pool, state_ret, page_table, w_ada, b_ada, norm_pre, norm_post, ffn_w_in, ffn_w_out, w_in_even, sb_bias, pool_w, pool_scale, w_out_even, w_in_odd, w_out_odd):
    modp, mods = _modulation(c_prompt, c_sample, w_ada, b_ada)
    bf = lambda w: w.astype(BF16)
    (y_p, kp, vp, pp, rp), (y_s, ks, vs, ps, rs) = _trunks(
        x_prompt, x_sample, modp, mods, cache_k, cache_v, state_pool, state_ret, page_table, norm_pre, norm_post,
        bf(ffn_w_in), bf(ffn_w_out), bf(w_in_even), sb_bias, bf(pool_w), pool_scale, bf(w_out_even), bf(w_in_odd),
        bf(w_out_odd))
    return (y_p, y_s, kp, vp, pp, rp, ks, vs, ps, rs)
```

```python
import functools

import numpy as np
import jax
import jax.numpy as jnp
from jax import lax
from jax.experimental import pallas as pl
from jax.experimental.pallas import tpu as pltpu

F32 = jnp.float32
BF16 = jnp.bfloat16

D_MODEL = 1024
D_FF = 2816
N_SUB = 3
NORM_EPS = 1e-6
SB_HEADS = 8
SB_HEAD_DIM = 64
SB_WIDTH = SB_HEADS * SB_HEAD_DIM
N_PAIRS = SB_HEADS // 2
POOL_WINDOWS = (2, 4, 8, 16)
POOL_GROUP = 128
POOL_WIDTH = len(POOL_WINDOWS) * POOL_GROUP
POOL_HIST = 16
RET_HEADS = 4
RET_QK_DIM = 256
RET_V_DIM = 512
RET_QK_WIDTH = RET_HEADS * RET_QK_DIM
RET_V_WIDTH = RET_HEADS * RET_V_DIM
RET_CHUNK = 256
ROPE_BASE = 10000.0
LN_EPS = 1e-5
LANES = 128
KBLK = 256
SUFFIX_PAD = 16
LOG2E = 1.4426950408889634
MASKED_LOGIT = -1e30
VMEM_LIMIT_BYTES = 56 * 1024 * 1024


def _params(*sem):
    return pltpu.CompilerParams(dimension_semantics=sem, vmem_limit_bytes=VMEM_LIMIT_BYTES)


def _resident(shape):
    nd = len(shape)
    return pl.BlockSpec(shape, lambda *_: (0,) * nd, pipeline_mode=pl.Buffered(1))


def _rms(x, g):
    ms = jnp.mean(x * x, axis=-1, keepdims=True)
    return x * lax.rsqrt(ms + NORM_EPS) * g


def _adaln_pre(x, m_ref, g):
    return _rms(x, g) * (1.0 + m_ref[0, 1]) + m_ref[0, 0]


def _silu(x):
    return x * jax.nn.sigmoid(x)


def _dot(a, b):
    return jnp.dot(a, b, preferred_element_type=F32)


def _dot_nt(a, b):
    return lax.dot_general(a, b, (((1,), (1,)), ((), ())), preferred_element_type=F32)


def _dot_tn(a, b):
    return lax.dot_general(a, b, (((0,), (0,)), ((), ())), preferred_element_type=F32)


def _mod_kernel(cp_ref, cs_ref, w_ref, b_ref, mp_ref, ms_ref):
    w = w_ref[0].astype(BF16)
    b = b_ref[0, 0]
    mp_ref[0, 0] = _dot(_silu(cp_ref[...]).astype(BF16), w) + b
    ms_ref[0, 0] = _dot(_silu(cs_ref[...]).astype(BF16), w) + b


def _modulation(c_p, c_s, w_ada, b_ada):
    depth = w_ada.shape[0]
    nb = N_SUB * 3
    bp, bs = c_p.shape[0], c_s.shape[0]
    return pl.pallas_call(
        _mod_kernel,
        grid=(depth, nb),
        in_specs=[
            pl.BlockSpec((bp, D_MODEL), lambda l, n: (0, 0)),
            pl.BlockSpec((bs, D_MODEL), lambda l, n: (0, 0)),
            pl.BlockSpec((1, D_MODEL, D_MODEL), lambda l, n: (l, 0, n)),
            pl.BlockSpec((1, 1, 1, D_MODEL), lambda l, n: (l, n, 0, 0)),
        ],
        out_specs=[
            pl.BlockSpec((1, 1, bp, D_MODEL), lambda l, n: (l, n, 0, 0)),
            pl.BlockSpec((1, 1, bs, D_MODEL), lambda l, n: (l, n, 0, 0)),
        ],
        out_shape=[
            jax.ShapeDtypeStruct((depth, nb, bp, D_MODEL), F32),
            jax.ShapeDtypeStruct((depth, nb, bs, D_MODEL), F32),
        ],
        compiler_params=_params("arbitrary", "arbitrary"),
        name="adaln_modulation",
    )(c_p, c_s, w_ada, b_ada.reshape(depth, nb, 1, D_MODEL))


def _ffn_kernel(x_ref, m_ref, gpre_ref, gpost_ref, win_ref, wout_ref, o_ref):
    x = x_ref[0]
    hb = _adaln_pre(x, m_ref, gpre_ref[...]).astype(BF16)
    gu = _dot(hb, win_ref[...])
    a = (_silu(gu[:, :D_FF]) * gu[:, D_FF:]).astype(BF16)
    y = _dot(a, wout_ref[...])
    o_ref[0] = x + 0.5 * m_ref[0, 2] * _rms(y, gpost_ref[...])


def _resident_member(stack, index):
    lead = len(index)
    shape = (None,) * lead + tuple(stack.shape[lead:])
    tail = (0,) * (stack.ndim - lead)
    return pl.BlockSpec(shape, lambda *_: tuple(index) + tail, pipeline_mode=pl.Buffered(1))


def _ffn_sublayer(x, mod, g_pre, g_post, w_in, w_out, member, tm):
    g, t, _ = x.shape
    r = mod.shape[2]
    rb = 1 if r == 1 else tm
    return pl.pallas_call(
        _ffn_kernel,
        grid=(g, t // tm),
        in_specs=[
            pl.BlockSpec((1, tm, D_MODEL), lambda b, j: (b, j, 0)),
            pl.BlockSpec((1, 3, rb, D_MODEL), (lambda b, j: (b, 0, 0, 0)) if r == 1 else (lambda b, j: (b, 0, j, 0))),
            _resident((1, D_MODEL)),
            _resident((1, D_MODEL)),
            _resident_member(w_in, member),
            _resident_member(w_out, member),
        ],
        out_specs=pl.BlockSpec((1, tm, D_MODEL), lambda b, j: (b, j, 0)),
        out_shape=jax.ShapeDtypeStruct(x.shape, F32),
        compiler_params=_params("arbitrary", "arbitrary"),
        name="ffn_sublayer",
    )(x, mod, g_pre, g_post, w_in, w_out)


def _softplus(z):
    return jnp.maximum(z, 0.0) + jnp.log(1.0 + jnp.exp(-jnp.abs(z)))


def _even_prompt_kernel(bias_ref, x_ref, m_ref, gpre_ref, gpost_ref, win_ref, pw_ref, ps_ref, wout_ref,
                        xo_ref, k_ref, v_ref, tail_ref,
                        kbuf, vtbuf, ubuf, obuf, acc_ref, car_ref, *, tm, tq):
    j = pl.program_id(1)
    nj = pl.num_programs(1)
    x = x_ref[0]
    hb = _adaln_pre(x, m_ref, gpre_ref[...]).astype(BF16)
    qk = _dot(hb, win_ref[:, 0:2 * SB_WIDTH])
    q = qk[:, 0:SB_WIDTH] * (SB_HEAD_DIM ** -0.5)
    k = qk[:, SB_WIDTH:]
    nsub = tm // KBLK
    for s in range(nsub):
        kbuf[j * nsub + s] = k[s * KBLK:(s + 1) * KBLK].astype(BF16)

    @pl.when(j == 0)
    def _():
        ubuf[0:POOL_HIST] = jnp.zeros((POOL_HIST, POOL_WIDTH), F32)

    def project_values():
        k_ref[0] = k.T
        vu = _dot(hb, win_ref[:, 2 * SB_WIDTH:])
        v_t = vu[:, 0:SB_WIDTH].T
        v_ref[0] = v_t
        for s in range(nsub):
            vtbuf[j * nsub + s] = v_t[:, s * KBLK:(s + 1) * KBLK].astype(BF16)
        ubuf[POOL_HIST:POOL_HIST + tm] = vu[:, SB_WIDTH:]

    lane = lax.broadcasted_iota(jnp.int32, (tq, 2 * SB_HEAD_DIM), 1)
    krow = lax.broadcasted_iota(jnp.int32, (KBLK, 2 * tq), 0)
    qcol = lax.broadcasted_iota(jnp.int32, (KBLK, 2 * tq), 1)
    qcol = jnp.where(qcol >= tq, qcol - tq, qcol)
    first_head = lax.broadcasted_iota(jnp.int32, (1, 2 * tq), 1) < tq
    mr = lax.broadcasted_iota(jnp.int32, (KBLK + SUFFIX_PAD, 2 * KBLK), 0)
    mc = lax.broadcasted_iota(jnp.int32, (KBLK + SUFFIX_PAD, 2 * KBLK), 1)
    mc = jnp.where(mc >= KBLK, mc - KBLK, mc)
    suffix_neg = jnp.where((mc > mr) | (mr == KBLK), -1.0, 0.0).astype(BF16)

    for qt in range(tm // tq):
        qtile = j * (tm // tq) + qt
        qrows = q[qt * tq:(qt + 1) * tq]
        qs_t, bias_row = [], []
        for p in range(N_PAIRS):
            q2 = qrows[:, 128 * p:128 * (p + 1)]
            qa = jnp.where(lane < SB_HEAD_DIM, q2, 0.0).T
            qb = jnp.where(lane >= SB_HEAD_DIM, q2, 0.0).T
            qs_t.append(jnp.concatenate([qa, qb], axis=1).astype(BF16))
            bias_row.append(jnp.where(first_head, bias_ref[2 * p], bias_ref[2 * p + 1]) * LOG2E)
        acc_ref[...] = jnp.zeros_like(acc_ref)
        car_ref[...] = jnp.zeros_like(car_ref)

        def step(kb, masked, after_scores=None):
            kblk = kbuf[kb]
            zs = [_dot(kblk[:, 128 * p:128 * (p + 1)], qs_t[p]) for p in range(N_PAIRS)]
            if after_scores is not None:
                after_scores()
            if masked:
                hide = jnp.where((kb * KBLK + krow) < (qtile * tq + qcol), 0.0, MASKED_LOGIT)
            logbs, hls = [], []
            for p in range(N_PAIRS):
                z = zs[p] * LOG2E + bias_row[p]
                if masked:
                    z = z + hide
                sp = jnp.maximum(z, 0.0) + jnp.log2(1.0 + jnp.exp2(-jnp.abs(z)))
                logbs.append(z - sp)
                hi = sp.astype(BF16)
                lo = (sp - hi.astype(F32)).astype(BF16)
                hls.append(jnp.concatenate([hi, lo], axis=0))
            sums = [_dot(suffix_neg, hls[p]) for p in range(N_PAIRS)]
            ws = []
            for p in range(N_PAIRS):
                ws.append(jnp.exp2(logbs[p] + sums[p][0:KBLK] + car_ref[p]).astype(BF16))
                car_ref[p] += sums[p][KBLK:KBLK + 1]
            vtb = vtbuf[kb]
            for p in range(N_PAIRS):
                acc_ref[p] += _dot(vtb[128 * p:128 * (p + 1), :], ws[p])

        ndiag = tq // KBLK
        n_full = qtile * ndiag
        for dk in range(ndiag - 1, -1, -1):
            step(n_full + dk, True, project_values if dk == ndiag - 1 else None)

        def body(it, carry):
            step(n_full - 1 - it, False)
            return carry

        lax.fori_loop(0, n_full, body, 0)

        for p in range(N_PAIRS):
            a = acc_ref[p]
            o_t = jnp.concatenate([a[0:SB_HEAD_DIM, 0:tq], a[SB_HEAD_DIM:, tq:]], axis=0)
            obuf[qt * tq:(qt + 1) * tq, 128 * p:128 * (p + 1)] = o_t.T.astype(BF16)

    y = _dot(obuf[:, 0:SB_WIDTH], wout_ref[0:SB_WIDTH, :])

    pos1 = (j * tm + 1 + lax.broadcasted_iota(jnp.int32, (tm, POOL_GROUP), 0)).astype(F32)
    for gi, wdw in enumerate(POOL_WINDOWS):
        lo_, hi_ = gi * POOL_GROUP, (gi + 1) * POOL_GROUP
        ug = ubuf[POOL_HIST:POOL_HIST + tm, lo_:hi_]
        ssum = ug
        for sft in range(1, wdw):
            ssum = ssum + ubuf[POOL_HIST - sft:POOL_HIST - sft + tm, lo_:hi_]
        d = ssum / jnp.minimum(pos1, float(wdw)) - ug
        yg = _dot(d.astype(BF16), pw_ref[gi]) * ps_ref[:, lo_:hi_]
        obuf[:, SB_WIDTH + lo_:SB_WIDTH + hi_] = yg.astype(BF16)
    hist = ubuf[tm:tm + POOL_HIST]
    ubuf[0:POOL_HIST] = hist

    @pl.when(j == nj - 1)
    def _():
        tail_ref[0] = hist

    y = y + _dot(obuf[:, SB_WIDTH:], wout_ref[SB_WIDTH:, :])
    xo_ref[0] = x + m_ref[0, 2] * _rms(y, gpost_ref[...])


def _even_prompt(x, mod, g_pre, g_post, w_in, sb_bias, pool_w, pool_scale, w_out, tm, tq):
    assert tm == tq and tq % KBLK == 0
    b, t, _ = x.shape
    nkb = t // KBLK
    grid_spec = pltpu.PrefetchScalarGridSpec(
        num_scalar_prefetch=1,
        grid=(b, t // tm),
        in_specs=[
            pl.BlockSpec((1, tm, D_MODEL), lambda i, j, *_: (i, j, 0)),
            pl.BlockSpec((1, 3, 1, D_MODEL), lambda i, j, *_: (i, 0, 0, 0)),
            _resident((1, D_MODEL)),
            _resident((1, D_MODEL)),
            _resident(w_in.shape),
            _resident(pool_w.shape),
            _resident((1, POOL_WIDTH)),
            _resident(w_out.shape),
        ],
        out_specs=[
            pl.BlockSpec((1, tm, D_MODEL), lambda i, j, *_: (i, j, 0)),
            pl.BlockSpec((1, SB_WIDTH, tm), lambda i, j, *_: (i, 0, j)),
            pl.BlockSpec((1, SB_WIDTH, tm), lambda i, j, *_: (i, 0, j)),
            pl.BlockSpec((1, POOL_HIST, POOL_WIDTH), lambda i, j, *_: (i, 0, 0)),
        ],
        scratch_shapes=[
            pltpu.VMEM((nkb, KBLK, SB_WIDTH), BF16),
            pltpu.VMEM((nkb, SB_WIDTH, KBLK), BF16),
            pltpu.VMEM((POOL_HIST + tm, POOL_WIDTH), F32),
            pltpu.VMEM((tm, SB_WIDTH + POOL_WIDTH), BF16),
            pltpu.VMEM((N_PAIRS, 2 * SB_HEAD_DIM, 2 * tq), F32),
            pltpu.VMEM((N_PAIRS, 1, 2 * tq), F32),
        ],
    )
    return pl.pallas_call(
        functools.partial(_even_prompt_kernel, tm=tm, tq=tq),
        grid_spec=grid_spec,
        out_shape=[
            jax.ShapeDtypeStruct(x.shape, F32),
            jax.ShapeDtypeStruct((b, SB_WIDTH, t), F32),
            jax.ShapeDtypeStruct((b, SB_WIDTH, t), F32),
            jax.ShapeDtypeStruct((b, POOL_HIST, POOL_WIDTH), F32),
        ],
        compiler_params=_params("arbitrary", "arbitrary"),
        name="even_prompt",
    )(sb_bias, x, mod, g_pre, g_post, w_in, pool_w, pool_scale, w_out)


def _ret_log_gamma():
    return np.log(np.float32(1.0) - np.exp2(np.float32(-5.0) - np.arange(RET_HEADS, dtype=np.float32))).astype(np.float32)


def _ret_tables(chunk):
    lg = jnp.asarray(_ret_log_gamma())
    i = jnp.arange(chunk, dtype=F32)
    diff = i[:, None] - i[None, :]
    decay = jnp.where(diff[None] >= 0, jnp.exp(jnp.maximum(diff, 0.0)[None] * lg[:, None, None]), 0.0)
    cross = jnp.exp((i[None, :] + 1.0) * lg[:, None])[:, :, None]
    kdec = jnp.exp((chunk - 1.0 - i)[None, :] * lg[:, None])[:, :, None]
    full = jnp.exp(chunk * lg)
    return decay, cross, kdec, full


def _rope_tables(pos):
    half = RET_QK_DIM // 2
    inv = ROPE_BASE ** (-jnp.arange(half, dtype=F32) / half)
    ang = pos[:, None] * inv[None, :]
    return jnp.cos(ang), jnp.sin(ang)


def _rotate(x, cos, sin):
    half = RET_QK_DIM // 2
    x1, x2 = x[:, :half], x[:, half:]
    return jnp.concatenate([x1 * cos - x2 * sin, x1 * sin + x2 * cos], axis=-1)


def _group_norm_gate(o, g):
    mu = jnp.mean(o, axis=-1, keepdims=True)
    c = o - mu
    var = jnp.mean(c * c, axis=-1, keepdims=True)
    return c * lax.rsqrt(var + LN_EPS) * _silu(g)


def _odd_prompt_kernel(full_ref, x_ref, m_ref, gpre_ref, gpost_ref, cos_ref, sin_ref, win_ref, wout_ref,
                       dec_ref, crs_ref, kdc_ref, xo_ref, s_ref, obuf, *, tm):
    j = pl.program_id(1)

    @pl.when(j == 0)
    def _():
        s_ref[...] = jnp.zeros_like(s_ref)

    x = x_ref[0]
    hb = _adaln_pre(x, m_ref, gpre_ref[...]).astype(BF16)
    cos, sin = cos_ref[...], sin_ref[...]
    qrs, krs, vhs, ghs = [], [], [], []
    for hd in range(RET_HEADS):
        qo = hd * RET_QK_DIM
        ko = RET_QK_WIDTH + hd * RET_QK_DIM
        vo = 2 * RET_QK_WIDTH + hd * RET_V_DIM
        go = 2 * RET_QK_WIDTH + RET_V_WIDTH + hd * RET_V_DIM
        qrs.append(_rotate(_dot(hb, win_ref[:, qo:qo + RET_QK_DIM]), cos, sin).astype(BF16))
        krs.append(_rotate(_dot(hb, win_ref[:, ko:ko + RET_QK_DIM]), cos, sin) * (RET_QK_DIM ** -0.5))
        vhs.append(_dot(hb, win_ref[:, vo:vo + RET_V_DIM]).astype(BF16))
        ghs.append(_dot(hb, win_ref[:, go:go + RET_V_DIM]))
    for c in range(tm // RET_CHUNK):
        rows = slice(c * RET_CHUNK, (c + 1) * RET_CHUNK)
        for hd in range(RET_HEADS):
            qc = qrs[hd][rows]
            kc = krs[hd][rows]
            vc = vhs[hd][rows]
            scores = _dot_nt(qc, kc.astype(BF16)) * dec_ref[hd]
            state = s_ref[0, hd]
            o = _dot(scores.astype(BF16), vc) + _dot(qc, state.astype(BF16)) * crs_ref[hd]
            kd = (kc * kdc_ref[hd]).astype(BF16)
            s_ref[0, hd] = full_ref[hd] * state + _dot_tn(kd, vc)
            obuf[rows, hd * RET_V_DIM:(hd + 1) * RET_V_DIM] = _group_norm_gate(o, ghs[hd][rows]).astype(BF16)
    y = _dot(obuf[...], wout_ref[...])
    xo_ref[0] = x + m_ref[0, 2] * _rms(y, gpost_ref[...])


def _odd_prompt(x, mod, g_pre, g_post, w_in, w_out, tm):
    b, t, _ = x.shape
    cos, sin = _rope_tables(jnp.arange(t, dtype=F32))
    decay, cross, kdec, full = _ret_tables(RET_CHUNK)
    half = RET_QK_DIM // 2
    grid_spec = pltpu.PrefetchScalarGridSpec(
        num_scalar_prefetch=1,
        grid=(b, t // tm),
        in_specs=[
            pl.BlockSpec((1, tm, D_MODEL), lambda i, j, *_: (i, j, 0)),
            pl.BlockSpec((1, 3, 1, D_MODEL), lambda i, j, *_: (i, 0, 0, 0)),
            _resident((1, D_MODEL)),
            _resident((1, D_MODEL)),
            pl.BlockSpec((tm, half), lambda i, j, *_: (j, 0)),
            pl.BlockSpec((tm, half), lambda i, j, *_: (j, 0)),
            _resident(w_in.shape),
            _resident(w_out.shape),
            _resident(decay.shape),
            _resident(cross.shape),
            _resident(kdec.shape),
        ],
        out_specs=[
            pl.BlockSpec((1, tm, D_MODEL), lambda i, j, *_: (i, j, 0)),
            pl.BlockSpec((1, RET_HEADS, RET_QK_DIM, RET_V_DIM), lambda i, j, *_: (i, 0, 0, 0)),
        ],
        scratch_shapes=[pltpu.VMEM((tm, RET_V_WIDTH), BF16)],
    )
    return pl.pallas_call(
        functools.partial(_odd_prompt_kernel, tm=tm),
        grid_spec=grid_spec,
        out_shape=[
            jax.ShapeDtypeStruct(x.shape, F32),
            jax.ShapeDtypeStruct((b, RET_HEADS, RET_QK_DIM, RET_V_DIM), F32),
        ],
        compiler_params=_params("arbitrary", "arbitrary"),
        name="odd_prompt",
    )(full, x, mod, g_pre, g_post, cos, sin, w_in, w_out, decay, cross, kdec)


def _even_sample_pre_kernel(x_ref, m_ref, gpre_ref, win_ref, pool_ref, pw_ref, ps_ref,
                            q_ref, k_ref, v_ref, pool_out_ref, opool_ref):
    hb = _adaln_pre(x_ref[...], m_ref, gpre_ref[...]).astype(BF16)
    qkvu = _dot(hb, win_ref[...])
    q_ref[...] = qkvu[:, 0:SB_WIDTH] * (SB_HEAD_DIM ** -0.5)
    k_ref[...] = qkvu[:, SB_WIDTH:2 * SB_WIDTH].T
    v_ref[...] = qkvu[:, 2 * SB_WIDTH:3 * SB_WIDTH].T
    u = qkvu[:, 3 * SB_WIDTH:]
    nh = POOL_HIST - 1
    for gi, wdw in enumerate(POOL_WINDOWS):
        lo_, hi_ = gi * POOL_GROUP, (gi + 1) * POOL_GROUP
        ug = u[:, lo_:hi_]
        ssum = ug
        for back in range(1, wdw):
            ssum = ssum + pool_ref[nh - back, :, lo_:hi_]
        d = ssum / float(wdw) - ug
        opool_ref[:, lo_:hi_] = (_dot(d.astype(BF16), pw_ref[gi]) * ps_ref[:, lo_:hi_]).astype(BF16)
    pool_out_ref[0:nh - 1] = pool_ref[1:nh]
    pool_out_ref[nh - 1] = u


def _even_sample_pre(x, mod, g_pre, w_in, pool_hist, pool_w, pool_scale):
    rows = x.shape[0]
    outs = [
        jax.ShapeDtypeStruct((rows, SB_WIDTH), F32),
        jax.ShapeDtypeStruct((SB_WIDTH, rows), F32),
        jax.ShapeDtypeStruct((SB_WIDTH, rows), F32),
        jax.ShapeDtypeStruct(pool_hist.shape, F32),
        jax.ShapeDtypeStruct((rows, POOL_WIDTH), BF16),
    ]
    return pl.pallas_call(
        _even_sample_pre_kernel,
        out_shape=outs,
        compiler_params=pltpu.CompilerParams(vmem_limit_bytes=VMEM_LIMIT_BYTES),
        name="even_sample_pre",
    )(x, mod, g_pre, w_in, pool_hist, pool_w, pool_scale)


def _sb_decode_kernel(pt_ref, bias_ref, q_ref, ck_ref, cv_ref, o_ref, kbuf, vbuf, sem, *, n_pages):
    i = pl.program_id(0)
    n = pl.num_programs(0)

    def copies(sample, slot):
        out = []
        for p in range(n_pages):
            pg = pt_ref[sample, p]
            out.append(pltpu.make_async_copy(ck_ref.at[pg], kbuf.at[slot, p], sem.at[0, slot]))
            out.append(pltpu.make_async_copy(cv_ref.at[pg], vbuf.at[slot, p], sem.at[1, slot]))
        return out

    @pl.when(i == 0)
    def _():
        for c in copies(0, 0):
            c.start()

    @pl.when(i + 1 < n)
    def _():
        for c in copies(i + 1, (i + 1) % 2):
            c.start()

    slot = i % 2
    for c in copies(i, slot):
        c.wait()

    q = q_ref[0]
    head_of_lane = lax.broadcasted_iota(jnp.int32, (SB_HEADS, SB_WIDTH), 1) // SB_HEAD_DIM
    own = head_of_lane == lax.broadcasted_iota(jnp.int32, (SB_HEADS, SB_WIDTH), 0)
    qblk = jnp.where(own, q, 0.0).astype(BF16)
    hrow = lax.broadcasted_iota(jnp.int32, (SB_HEADS, 1), 0)
    bias = jnp.zeros((SB_HEADS, 1), F32)
    for h in range(SB_HEADS):
        bias = jnp.where(hrow == h, bias_ref[h], bias)
    sps, logbs = [], []
    for p in range(n_pages):
        z = _dot(qblk, kbuf[slot, p].astype(BF16)) + bias
        sp = _softplus(z)
        sps.append(sp)
        logbs.append(z - sp)
    page = sps[0].shape[1]
    st = jnp.concatenate(sps, axis=0)
    hi = st.astype(BF16)
    lo = (st - hi.astype(F32)).astype(BF16)
    hl = jnp.concatenate([hi, lo], axis=1)
    jr = lax.broadcasted_iota(jnp.int32, (2 * page, page), 0)
    jr = jnp.where(jr >= page, jr - page, jr)
    sc = lax.broadcasted_iota(jnp.int32, (2 * page, page), 1)
    within = _dot(hl, jnp.where(jr > sc, -1.0, 0.0).astype(BF16))
    total = _dot(hl, jnp.full((2 * page, page), -1.0, BF16))
    carry = jnp.zeros((SB_HEADS, page), F32)
    o8 = jnp.zeros((SB_HEADS, SB_WIDTH), F32)
    for p in range(n_pages - 1, -1, -1):
        rows = slice(p * SB_HEADS, (p + 1) * SB_HEADS)
        w = jnp.exp(logbs[p] + within[rows] + carry).astype(BF16)
        carry = carry + total[rows]
        o8 = o8 + _dot_nt(w, vbuf[slot, p].astype(BF16))
    o_ref[0] = jnp.sum(jnp.where(own, o8, 0.0), axis=0, keepdims=True)


def _sb_decode(q, sb_bias, page_table, cache_k, cache_v):
    rows, n_pages = page_table.shape
    page = cache_k.shape[2]
    grid_spec = pltpu.PrefetchScalarGridSpec(
        num_scalar_prefetch=2,
        grid=(rows,),
        in_specs=[
            pl.BlockSpec((1, 1, SB_WIDTH), lambda i, *_: (i, 0, 0)),
            pl.BlockSpec(memory_space=pl.ANY),
            pl.BlockSpec(memory_space=pl.ANY),
        ],
        out_specs=pl.BlockSpec((1, 1, SB_WIDTH), lambda i, *_: (i, 0, 0)),
        scratch_shapes=[
            pltpu.VMEM((2, n_pages, SB_WIDTH, page), F32),
            pltpu.VMEM((2, n_pages, SB_WIDTH, page), F32),
            pltpu.SemaphoreType.DMA((2, 2)),
        ],
    )
    out = pl.pallas_call(
        functools.partial(_sb_decode_kernel, n_pages=n_pages),
        grid_spec=grid_spec,
        out_shape=jax.ShapeDtypeStruct((rows, 1, SB_WIDTH), F32),
        compiler_params=_params("arbitrary"),
        name="sb_decode",
    )(page_table, sb_bias, q.reshape(rows, 1, SB_WIDTH), cache_k, cache_v)
    return out.reshape(rows, SB_WIDTH)


def _even_sample_post_kernel(x_ref, m_ref, gpost_ref, osb_ref, opool_ref, wout_ref, xo_ref):
    y = _dot(osb_ref[...].astype(BF16), wout_ref[0:SB_WIDTH, :]) + _dot(opool_ref[...], wout_ref[SB_WIDTH:, :])
    xo_ref[...] = x_ref[...] + m_ref[0, 2] * _rms(y, gpost_ref[...])


def _even_sample_post(x, mod, g_post, o_sb, o_pool, w_out):
    return pl.pallas_call(
        _even_sample_post_kernel,
        out_shape=jax.ShapeDtypeStruct(x.shape, F32),
        compiler_params=pltpu.CompilerParams(vmem_limit_bytes=VMEM_LIMIT_BYTES),
        name="even_sample_post",
    )(x, mod, g_post, o_sb, o_pool, w_out)


def _odd_sample_pre_kernel(x_ref, m_ref, gpre_ref, cos_ref, sin_ref, win_ref, q_ref, k_ref, v_ref, g_ref):
    hb = _adaln_pre(x_ref[...], m_ref, gpre_ref[...]).astype(BF16)
    cos, sin = cos_ref[...], sin_ref[...]
    for hd in range(RET_HEADS):
        qo = hd * RET_QK_DIM
        ko = RET_QK_WIDTH + hd * RET_QK_DIM
        q_ref[:, qo:qo + RET_QK_DIM] = _rotate(_dot(hb, win_ref[:, qo:qo + RET_QK_DIM]), cos, sin)
        k_ref[:, qo:qo + RET_QK_DIM] = _rotate(_dot(hb, win_ref[:, ko:ko + RET_QK_DIM]), cos, sin) * (RET_QK_DIM ** -0.5)
    vo = 2 * RET_QK_WIDTH
    v_ref[...] = _dot(hb, win_ref[:, vo:vo + RET_V_WIDTH])
    g_ref[...] = _dot(hb, win_ref[:, vo + RET_V_WIDTH:])


def _odd_sample_pre(x, mod, g_pre, cos, sin, w_in):
    rows = x.shape[0]
    outs = [
        jax.ShapeDtypeStruct((rows, RET_QK_WIDTH), F32),
        jax.ShapeDtypeStruct((rows, RET_QK_WIDTH), F32),
        jax.ShapeDtypeStruct((rows, RET_V_WIDTH), F32),
        jax.ShapeDtypeStruct((rows, RET_V_WIDTH), F32),
    ]
    return pl.pallas_call(
        _odd_sample_pre_kernel,
        out_shape=outs,
        compiler_params=pltpu.CompilerParams(vmem_limit_bytes=VMEM_LIMIT_BYTES),
        name="odd_sample_pre",
    )(x, mod, g_pre, cos, sin, w_in)


def _retention_token(gam_ref, q, k, v, s_ref, so_ref):
    row0 = lax.broadcasted_iota(jnp.int32, (16, RET_QK_DIM), 0) == 0
    outs = []
    for hd in range(RET_HEADS):
        qh = q[:, hd * RET_QK_DIM:(hd + 1) * RET_QK_DIM]
        kh = k[:, hd * RET_QK_DIM:(hd + 1) * RET_QK_DIM]
        vh = v[:, hd * RET_V_DIM:(hd + 1) * RET_V_DIM]
        qb = qh.astype(BF16)
        vb = vh.astype(BF16).astype(F32)
        score = jnp.sum(qb.astype(F32) * kh.astype(BF16).astype(F32), axis=-1, keepdims=True)
        state = s_ref[hd]
        q16 = jnp.where(row0, jnp.broadcast_to(qh, (16, RET_QK_DIM)), 0.0).astype(BF16)
        cross = _dot(q16, state.astype(BF16))[0:1] * gam_ref[hd]
        outs.append(score.astype(BF16).astype(F32) * vb + cross)
        kcol = jnp.broadcast_to(kh, (8, RET_QK_DIM)).T[:, 0:1]
        so_ref[hd] = gam_ref[hd] * state + kcol * vh
    return jnp.concatenate(outs, axis=-1)


def _ffn_ret_kernel(gam_ref, x_ref, m_ref, gpre_ref, gpost_ref, win_ref, wout_ref, q_ref, k_ref, v_ref, s_hbm,
                    xo_ref, o_ref, so_hbm, s_in, s_out, sem_in, sem_out, *, nb):
    t = pl.program_id(0) * pl.num_programs(1) + pl.program_id(1)
    n = pl.num_programs(0) * pl.num_programs(1)
    slot = t % 2

    def load(step, sl):
        return pltpu.make_async_copy(s_hbm.at[pl.ds(step * nb, nb)], s_in.at[sl], sem_in.at[sl])

    def store(step, sl):
        return pltpu.make_async_copy(s_out.at[sl], so_hbm.at[pl.ds(step * nb, nb)], sem_out.at[sl])

    @pl.when(t == 0)
    def _():
        load(0, 0).start()

    @pl.when(t + 1 < n)
    def _():
        load(t + 1, 1 - slot).start()

    load(t, slot).wait()

    @pl.when(t >= 2)
    def _():
        store(t - 2, slot).wait()

    for i in range(nb):
        b = t * nb + i
        o_ref[pl.ds(b, 1), :] = _retention_token(gam_ref, q_ref[pl.ds(b, 1), :], k_ref[pl.ds(b, 1), :],
                                                 v_ref[pl.ds(b, 1), :], s_in.at[slot, i], s_out.at[slot, i])
    store(t, slot).start()

    _ffn_kernel(x_ref, m_ref, gpre_ref, gpost_ref, win_ref, wout_ref, xo_ref)

    @pl.when(t == n - 1)
    def _():
        @pl.when(n >= 2)
        def _():
            store(t - 1, 1 - slot).wait()
        store(t, slot).wait()


def _ffn_sublayer_with_retention_step(x, mod, g_pre, g_post, w_in, w_out, member, q, k, v, state, tm):
    g, t, _ = x.shape
    rows = q.shape[0]
    steps = g * (t // tm)
    nb = rows // steps
    assert nb * steps == rows and mod.shape[2] == 1
    gamma = jnp.exp(jnp.asarray(_ret_log_gamma()))
    state_buf = pltpu.VMEM((2, nb, RET_HEADS, RET_QK_DIM, RET_V_DIM), F32)
    grid_spec = pltpu.PrefetchScalarGridSpec(
        num_scalar_prefetch=1,
        grid=(g, t // tm),
        in_specs=[
            pl.BlockSpec((1, tm, D_MODEL), lambda b, j, *_: (b, j, 0)),
            pl.BlockSpec((1, 3, 1, D_MODEL), lambda b, j, *_: (b, 0, 0, 0)),
            _resident((1, D_MODEL)),
            _resident((1, D_MODEL)),
            _resident_member(w_in, member),
            _resident_member(w_out, member),
            _resident(q.shape),
            _resident(k.shape),
            _resident(v.shape),
            pl.BlockSpec(memory_space=pl.ANY),
        ],
        out_specs=[
            pl.BlockSpec((1, tm, D_MODEL), lambda b, j, *_: (b, j, 0)),
            pl.BlockSpec((rows, RET_V_WIDTH), lambda b, j, *_: (0, 0)),
            pl.BlockSpec(memory_space=pl.ANY),
        ],
        scratch_shapes=[state_buf, state_buf, pltpu.SemaphoreType.DMA((2,)), pltpu.SemaphoreType.DMA((2,))],
    )
    return pl.pallas_call(
        functools.partial(_ffn_ret_kernel, nb=nb),
        grid_spec=grid_spec,
        out_shape=[
            jax.ShapeDtypeStruct(x.shape, F32),
            jax.ShapeDtypeStruct((rows, RET_V_WIDTH), F32),
            jax.ShapeDtypeStruct(state.shape, F32),
        ],
        compiler_params=_params("arbitrary", "arbitrary"),
        name="ffn_sublayer_retention_step",
    )(gamma, x, mod, g_pre, g_post, w_in, w_out, q, k, v, state)


def _odd_sample_post_kernel(x_ref, m_ref, gpost_ref, o_ref, g_ref, wout_ref, xo_ref, obuf):
    for hd in range(RET_HEADS):
        cols = slice(hd * RET_V_DIM, (hd + 1) * RET_V_DIM)
        obuf[:, cols] = _group_norm_gate(o_ref[:, cols], g_ref[:, cols]).astype(BF16)
    y = _dot(obuf[...], wout_ref[...])
    xo_ref[...] = x_ref[...] + m_ref[0, 2] * _rms(y, gpost_ref[...])


def _odd_sample_post(x, mod, g_post, o, g, w_out):
    return pl.pallas_call(
        _odd_sample_post_kernel,
        out_shape=jax.ShapeDtypeStruct(x.shape, F32),
        scratch_shapes=[pltpu.VMEM((x.shape[0], RET_V_WIDTH), BF16)],
        compiler_params=pltpu.CompilerParams(vmem_limit_bytes=VMEM_LIMIT_BYTES),
        name="odd_sample_post",
    )(x, mod, g_post, o, g, w_out)


PROMPT_FFN_ROWS = 512
PROMPT_HOST_ROWS = 256
PROMPT_EVEN_ROWS = 256
PROMPT_ODD_ROWS = 512


def _trunks(xp, xs, modp, mods, cache_k, cache_v, state_pool, state_ret, page_table, norm_pre, norm_post, ffn_w_in,
            ffn_w_out, w_in_even, sb_bias, pool_w, pool_scale, w_out_even, w_in_odd, w_out_odd):
    depth = modp.shape[0]
    b = xp.shape[0]
    rows = xs.shape[0]
    page = cache_k.shape[2]
    past_len = page_table.shape[1] * page
    xs = xs.reshape(rows, D_MODEL)
    out_p = dict(k=[], v=[], pool=[], ret=[])
    out_s = dict(k=[], v=[], pool=[], ret=[])
    for l in range(depth):
        li = l // 2
        mod_p = lambda s: modp[l, 3 * s:3 * s + 3].transpose(1, 0, 2).reshape(b, 3, 1, D_MODEL)
        mod_s = lambda s: mods[l, 3 * s:3 * s + 3].reshape(1, 3, rows, D_MODEL)
        gp = lambda s: norm_pre[l, s].reshape(1, D_MODEL)
        gq = lambda s: norm_post[l, s].reshape(1, D_MODEL)
        ffn = lambda x, mod, s, half, tm: _ffn_sublayer(x, mod, gp(s), gq(s), ffn_w_in, ffn_w_out, (l, half), tm=tm)

        xp = ffn(xp, mod_p(0), 0, 0, PROMPT_FFN_ROWS)
        xs = ffn(xs[None], mod_s(0), 0, 0, rows)[0]
        if l % 2 == 0:
            xp, k, v, tail = _even_prompt(xp, mod_p(1), gp(1), gq(1), w_in_even[li], sb_bias[li], pool_w[li],
                                          pool_scale[li].reshape(1, POOL_WIDTH), w_out_even[li],
                                          tm=PROMPT_EVEN_ROWS, tq=PROMPT_EVEN_ROWS)
            heads_last = lambda a: a.reshape(b, SB_HEADS, SB_HEAD_DIM, -1).transpose(0, 3, 1, 2)
            out_p["k"].append(heads_last(k))
            out_p["v"].append(heads_last(v))
            out_p["pool"].append(tail[:, 1:])

            pool_hist = state_pool[li].transpose(1, 0, 2)
            q, k_t, v_t, pool_new, o_pool = _even_sample_pre(xs, mod_s(1), gp(1), w_in_even[li], pool_hist, pool_w[li],
                                                             pool_scale[li].reshape(1, POOL_WIDTH))
            pages_t = lambda c: c[li].transpose(0, 2, 3, 1).reshape(-1, SB_WIDTH, page)
            o_sb = _sb_decode(q, sb_bias[li], page_table, pages_t(cache_k), pages_t(cache_v))
            xs = _even_sample_post(xs, mod_s(1), gq(1), o_sb, o_pool, w_out_even[li])
            sample_heads_last = lambda a: a.reshape(SB_HEADS, SB_HEAD_DIM, rows).transpose(2, 0, 1)[:, None]
            out_s["k"].append(sample_heads_last(k_t))
            out_s["v"].append(sample_heads_last(v_t))
            out_s["pool"].append(pool_new.transpose(1, 0, 2))
            xp = ffn(xp, mod_p(2), 2, 1, PROMPT_FFN_ROWS)
        else:
            xp, s = _odd_prompt(xp, mod_p(1), gp(1), gq(1), w_in_odd[li], w_out_odd[li], tm=PROMPT_ODD_ROWS)
            out_p["ret"].append(s)
            cos, sin = _rope_tables(jnp.full((1,), past_len, F32))
            q, k, v, g = _odd_sample_pre(xs, mod_s(1), gp(1), cos, sin, w_in_odd[li])
            xp, o, s_new = _ffn_sublayer_with_retention_step(xp, mod_p(2), gp(2), gq(2), ffn_w_in, ffn_w_out, (l, 1),
                                                             q, k, v, state_ret[li], tm=PROMPT_HOST_ROWS)
            xs = _odd_sample_post(xs, mod_s(1), gq(1), o, g, w_out_odd[li])
            out_s["ret"].append(s_new)
        xs = ffn(xs[None], mod_s(2), 2, 1, rows)[0]
    stacked = lambda o: tuple(jnp.stack(o[n], 0) for n in ("k", "v", "pool", "ret"))
    return (xp,) + stacked(out_p), (xs.reshape(rows, 1, D_MODEL),) + stacked(out_s)


def kernel(x_prompt, x_sample, c_prompt, c_sample, cache_k, cache_v, state_pool, state_ret, page_table, w_ada, b_ada, norm_pre, norm_post, ffn_w_in, ffn_w_out, w_in_even, sb_bias, pool_w, pool_scale, w_out_even, w_in_odd, w_out_odd):
    modp, mods = _modulation(c_prompt, c_sample, w_ada, b_ada)
    bf = lambda w: w.astype(BF16)
    (y_p, kp, vp, pp, rp), (y_s, ks, vs, ps, rs) = _trunks(
        x_prompt, x_sample, modp, mods, cache_k, cache_v, state_pool, state_ret, page_table, norm_pre, norm_post,
        bf(ffn_w_in), bf(ffn_w_out), bf(w_in_even), sb_bias, bf(pool_w), pool_scale, bf(w_out_even), bf(w_in_odd),
        bf(w_out_odd))
    return (y_p, y_s, kp, vp, pp, rp, ks, vs, ps, rs)
```

```python
import functools

import numpy as np
import jax
import jax.numpy as jnp
from jax import lax
from jax.experimental import pallas as pl
from jax.experimental.pallas import tpu as pltpu

F32 = jnp.float32
BF16 = jnp.bfloat16

D_MODEL = 1024
D_FF = 2816
N_SUB = 3
NORM_EPS = 1e-6
SB_HEADS = 8
SB_HEAD_DIM = 64
SB_WIDTH = SB_HEADS * SB_HEAD_DIM
N_PAIRS = SB_HEADS // 2
POOL_WINDOWS = (2, 4, 8, 16)
POOL_GROUP = 128
POOL_WIDTH = len(POOL_WINDOWS) * POOL_GROUP
POOL_HIST = 16
RET_HEADS = 4
RET_QK_DIM = 256
RET_V_DIM = 512
RET_QK_WIDTH = RET_HEADS * RET_QK_DIM
RET_V_WIDTH = RET_HEADS * RET_V_DIM
RET_CHUNK = 256
ROPE_BASE = 10000.0
LN_EPS = 1e-5
LANES = 128
KBLK = 256
SUFFIX_PAD = 16
LOG2E = 1.4426950408889634
MASKED_LOGIT = -1e30
VMEM_LIMIT_BYTES = 56 * 1024 * 1024


def _params(*sem):
    return pltpu.CompilerParams(dimension_semantics=sem, vmem_limit_bytes=VMEM_LIMIT_BYTES)


def _resident(shape):
    nd = len(shape)
    return pl.BlockSpec(shape, lambda *_: (0,) * nd, pipeline_mode=pl.Buffered(1))


def _rms(x, g):
    ms = jnp.mean(x * x, axis=-1, keepdims=True)
    return x * lax.rsqrt(ms + NORM_EPS) * g


def _adaln_pre(x, m_ref, g):
    return _rms(x, g) * (1.0 + m_ref[0, 1]) + m_ref[0, 0]


def _silu(x):
    return x * jax.nn.sigmoid(x)


def _dot(a, b):
    return jnp.dot(a, b, preferred_element_type=F32)


def _dot_nt(a, b):
    return lax.dot_general(a, b, (((1,), (1,)), ((), ())), preferred_element_type=F32)


def _dot_tn(a, b):
    return lax.dot_general(a, b, (((0,), (0,)), ((), ())), preferred_element_type=F32)


def _mod_kernel(cp_ref, cs_ref, w_ref, b_ref, mp_ref, ms_ref):
    w = w_ref[0].astype(BF16)
    b = b_ref[0, 0]
    mp_ref[0, 0] = _dot(_silu(cp_ref[...]).astype(BF16), w) + b
    ms_ref[0, 0] = _dot(_silu(cs_ref[...]).astype(BF16), w) + b


def _modulation(c_p, c_s, w_ada, b_ada):
    depth = w_ada.shape[0]
    nb = N_SUB * 3
    bp, bs = c_p.shape[0], c_s.shape[0]
    return pl.pallas_call(
        _mod_kernel,
        grid=(depth, nb),
        in_specs=[
            pl.BlockSpec((bp, D_MODEL), lambda l, n: (0, 0)),
            pl.BlockSpec((bs, D_MODEL), lambda l, n: (0, 0)),
            pl.BlockSpec((1, D_MODEL, D_MODEL), lambda l, n: (l, 0, n)),
            pl.BlockSpec((1, 1, 1, D_MODEL), lambda l, n: (l, n, 0, 0)),
        ],
        out_specs=[
            pl.BlockSpec((1, 1, bp, D_MODEL), lambda l, n: (l, n, 0, 0)),
            pl.BlockSpec((1, 1, bs, D_MODEL), lambda l, n: (l, n, 0, 0)),
        ],
        out_shape=[
            jax.ShapeDtypeStruct((depth, nb, bp, D_MODEL), F32),
            jax.ShapeDtypeStruct((depth, nb, bs, D_MODEL), F32),
        ],
        compiler_params=_params("arbitrary", "arbitrary"),
        name="adaln_modulation",
    )(c_p, c_s, w_ada, b_ada.reshape(depth, nb, 1, D_MODEL))


def _ffn_kernel(x_ref, m_ref, gpre_ref, gpost_ref, win_ref, wout_ref, o_ref):
    x = x_ref[0]
    hb = _adaln_pre(x, m_ref, gpre_ref[...]).astype(BF16)
    gu = _dot(hb, win_ref[...])
    a = (_silu(gu[:, :D_FF]) * gu[:, D_FF:]).astype(BF16)
    y = _dot(a, wout_ref[...])
    o_ref[0] = x + 0.5 * m_ref[0, 2] * _rms(y, gpost_ref[...])


def _resident_member(stack, index):
    lead = len(index)
    shape = (None,) * lead + tuple(stack.shape[lead:])
    tail = (0,) * (stack.ndim - lead)
    return pl.BlockSpec(shape, lambda *_: tuple(index) + tail, pipeline_mode=pl.Buffered(1))


def _ffn_sublayer(x, mod, g_pre, g_post, w_in, w_out, member, tm):
    g, t, _ = x.shape
    r = mod.shape[2]
    rb = 1 if r == 1 else tm
    return pl.pallas_call(
        _ffn_kernel,
        grid=(g, t // tm),
        in_specs=[
            pl.BlockSpec((1, tm, D_MODEL), lambda b, j: (b, j, 0)),
            pl.BlockSpec((1, 3, rb, D_MODEL), (lambda b, j: (b, 0, 0, 0)) if r == 1 else (lambda b, j: (b, 0, j, 0))),
            _resident((1, D_MODEL)),
            _resident((1, D_MODEL)),
            _resident_member(w_in, member),
            _resident_member(w_out, member),
        ],
        out_specs=pl.BlockSpec((1, tm, D_MODEL), lambda b, j: (b, j, 0)),
        out_shape=jax.ShapeDtypeStruct(x.shape, F32),
        compiler_params=_params("arbitrary", "arbitrary"),
        name="ffn_sublayer",
    )(x, mod, g_pre, g_post, w_in, w_out)


def _softplus(z):
    return jnp.maximum(z, 0.0) + jnp.log(1.0 + jnp.exp(-jnp.abs(z)))


def _even_prompt_kernel(bias_ref, pt_ref, x_ref, m_ref, gpre_ref, gpost_ref, win_ref, pw_ref, ps_ref, wout_ref,
                        qd_ref, ck_hbm, cv_hbm,
                        xo_ref, k_ref, v_ref, tail_ref, od_ref,
                        kbuf, vtbuf, ubuf, obuf, acc_ref, car_ref, kd, vd, dsem, *, tm, tq):
    j = pl.program_id(1)
    nj = pl.num_programs(1)
    step = pl.program_id(0) * nj + j
    n_steps = pl.num_programs(0) * nj

    def page_copies(sample, slot):
        out = []
        for p in range(kd.shape[1]):
            pg = pt_ref[sample, p]
            out.append(pltpu.make_async_copy(ck_hbm.at[pg], kd.at[slot, p], dsem.at[0, slot]))
            out.append(pltpu.make_async_copy(cv_hbm.at[pg], vd.at[slot, p], dsem.at[1, slot]))
        return out

    def decode(sample, slot):
        for c in page_copies(sample, slot):
            c.wait()
        od_ref[pl.ds(sample, 1), :] = _decode_sample(qd_ref[pl.ds(sample, 1), :], bias_ref, kd.at[slot], vd.at[slot])

    @pl.when(step == 0)
    def _():
        for c in page_copies(0, 0):
            c.start()

    for c in page_copies(2 * step + 1, 1):
        c.start()
    decode(2 * step, 0)

    x = x_ref[0]
    hb = _adaln_pre(x, m_ref, gpre_ref[...]).astype(BF16)
    qk = _dot(hb, win_ref[:, 0:2 * SB_WIDTH])
    q = qk[:, 0:SB_WIDTH] * (SB_HEAD_DIM ** -0.5)
    k = qk[:, SB_WIDTH:]
    nsub = tm // KBLK
    for s in range(nsub):
        kbuf[j * nsub + s] = k[s * KBLK:(s + 1) * KBLK].astype(BF16)

    @pl.when(j == 0)
    def _():
        ubuf[0:POOL_HIST] = jnp.zeros((POOL_HIST, POOL_WIDTH), F32)

    def project_values():
        k_ref[0] = k.T
        vu = _dot(hb, win_ref[:, 2 * SB_WIDTH:])
        v_t = vu[:, 0:SB_WIDTH].T
        v_ref[0] = v_t
        for s in range(nsub):
            vtbuf[j * nsub + s] = v_t[:, s * KBLK:(s + 1) * KBLK].astype(BF16)
        ubuf[POOL_HIST:POOL_HIST + tm] = vu[:, SB_WIDTH:]

    lane = lax.broadcasted_iota(jnp.int32, (tq, 2 * SB_HEAD_DIM), 1)
    krow = lax.broadcasted_iota(jnp.int32, (KBLK, 2 * tq), 0)
    qcol = lax.broadcasted_iota(jnp.int32, (KBLK, 2 * tq), 1)
    qcol = jnp.where(qcol >= tq, qcol - tq, qcol)
    first_head = lax.broadcasted_iota(jnp.int32, (1, 2 * tq), 1) < tq
    mr = lax.broadcasted_iota(jnp.int32, (KBLK + SUFFIX_PAD, 2 * KBLK), 0)
    mc = lax.broadcasted_iota(jnp.int32, (KBLK + SUFFIX_PAD, 2 * KBLK), 1)
    mc = jnp.where(mc >= KBLK, mc - KBLK, mc)
    suffix_neg = jnp.where((mc > mr) | (mr == KBLK), -1.0, 0.0).astype(BF16)

    for qt in range(tm // tq):
        qtile = j * (tm // tq) + qt
        qrows = q[qt * tq:(qt + 1) * tq]
        qs_t, bias_row = [], []
        for p in range(N_PAIRS):
            q2 = qrows[:, 128 * p:128 * (p + 1)]
            qa = jnp.where(lane < SB_HEAD_DIM, q2, 0.0).T
            qb = jnp.where(lane >= SB_HEAD_DIM, q2, 0.0).T
            qs_t.append(jnp.concatenate([qa, qb], axis=1).astype(BF16))
            bias_row.append(jnp.where(first_head, bias_ref[2 * p], bias_ref[2 * p + 1]) * LOG2E)
        acc_ref[...] = jnp.zeros_like(acc_ref)
        car_ref[...] = jnp.zeros_like(car_ref)

        def attend(kb, masked, after_scores=None):
            kblk = kbuf[kb]
            zs = [_dot(kblk[:, 128 * p:128 * (p + 1)], qs_t[p]) for p in range(N_PAIRS)]
            if after_scores is not None:
                after_scores()
            if masked:
                hide = jnp.where((kb * KBLK + krow) < (qtile * tq + qcol), 0.0, MASKED_LOGIT)
            logbs, hls = [], []
            for p in range(N_PAIRS):
                z = zs[p] * LOG2E + bias_row[p]
                if masked:
                    z = z + hide
                sp = jnp.maximum(z, 0.0) + jnp.log2(1.0 + jnp.exp2(-jnp.abs(z)))
                logbs.append(z - sp)
                hi = sp.astype(BF16)
                lo = (sp - hi.astype(F32)).astype(BF16)
                hls.append(jnp.concatenate([hi, lo], axis=0))
            sums = [_dot(suffix_neg, hls[p]) for p in range(N_PAIRS)]
            ws = []
            for p in range(N_PAIRS):
                ws.append(jnp.exp2(logbs[p] + sums[p][0:KBLK] + car_ref[p]).astype(BF16))
                car_ref[p] += sums[p][KBLK:KBLK + 1]
            vtb = vtbuf[kb]
            for p in range(N_PAIRS):
                acc_ref[p] += _dot(vtb[128 * p:128 * (p + 1), :], ws[p])

        ndiag = tq // KBLK
        n_full = qtile * ndiag
        for dk in range(ndiag - 1, -1, -1):
            attend(n_full + dk, True, project_values if dk == ndiag - 1 else None)

        def body(it, carry):
            attend(n_full - 1 - it, False)
            return carry

        lax.fori_loop(0, n_full, body, 0)

        for p in range(N_PAIRS):
            a = acc_ref[p]
            o_t = jnp.concatenate([a[0:SB_HEAD_DIM, 0:tq], a[SB_HEAD_DIM:, tq:]], axis=0)
            obuf[qt * tq:(qt + 1) * tq, 128 * p:128 * (p + 1)] = o_t.T.astype(BF16)

    @pl.when(step + 1 < n_steps)
    def _():
        for c in page_copies(2 * step + 2, 0):
            c.start()

    decode(2 * step + 1, 1)

    y = _dot(obuf[:, 0:SB_WIDTH], wout_ref[0:SB_WIDTH, :])

    pos1 = (j * tm + 1 + lax.broadcasted_iota(jnp.int32, (tm, POOL_GROUP), 0)).astype(F32)
    for gi, wdw in enumerate(POOL_WINDOWS):
        lo_, hi_ = gi * POOL_GROUP, (gi + 1) * POOL_GROUP
        ug = ubuf[POOL_HIST:POOL_HIST + tm, lo_:hi_]
        ssum = ug
        for sft in range(1, wdw):
            ssum = ssum + ubuf[POOL_HIST - sft:POOL_HIST - sft + tm, lo_:hi_]
        d = ssum / jnp.minimum(pos1, float(wdw)) - ug
        yg = _dot(d.astype(BF16), pw_ref[gi]) * ps_ref[:, lo_:hi_]
        obuf[:, SB_WIDTH + lo_:SB_WIDTH + hi_] = yg.astype(BF16)
    hist = ubuf[tm:tm + POOL_HIST]
    ubuf[0:POOL_HIST] = hist

    @pl.when(j == nj - 1)
    def _():
        tail_ref[0] = hist

    y = y + _dot(obuf[:, SB_WIDTH:], wout_ref[SB_WIDTH:, :])
    xo_ref[0] = x + m_ref[0, 2] * _rms(y, gpost_ref[...])


def _even_prompt(x, mod, g_pre, g_post, w_in, sb_bias, pool_w, pool_scale, w_out, q_dec, page_table, cache_k, cache_v,
                 tm, tq):
    assert tm == tq and tq % KBLK == 0
    b, t, _ = x.shape
    nkb = t // KBLK
    rows, n_pages = page_table.shape
    page = cache_k.shape[2]
    assert rows == 2 * b * (t // tm)
    grid_spec = pltpu.PrefetchScalarGridSpec(
        num_scalar_prefetch=2,
        grid=(b, t // tm),
        in_specs=[
            pl.BlockSpec((1, tm, D_MODEL), lambda i, j, *_: (i, j, 0)),
            pl.BlockSpec((1, 3, 1, D_MODEL), lambda i, j, *_: (i, 0, 0, 0)),
            _resident((1, D_MODEL)),
            _resident((1, D_MODEL)),
            _resident(w_in.shape),
            _resident(pool_w.shape),
            _resident((1, POOL_WIDTH)),
            _resident(w_out.shape),
            _resident(q_dec.shape),
            pl.BlockSpec(memory_space=pl.ANY),
            pl.BlockSpec(memory_space=pl.ANY),
        ],
        out_specs=[
            pl.BlockSpec((1, tm, D_MODEL), lambda i, j, *_: (i, j, 0)),
            pl.BlockSpec((1, SB_WIDTH, tm), lambda i, j, *_: (i, 0, j)),
            pl.BlockSpec((1, SB_WIDTH, tm), lambda i, j, *_: (i, 0, j)),
            pl.BlockSpec((1, POOL_HIST, POOL_WIDTH), lambda i, j, *_: (i, 0, 0)),
            pl.BlockSpec((rows, SB_WIDTH), lambda i, j, *_: (0, 0)),
        ],
        scratch_shapes=[
            pltpu.VMEM((nkb, KBLK, SB_WIDTH), BF16),
            pltpu.VMEM((nkb, SB_WIDTH, KBLK), BF16),
            pltpu.VMEM((POOL_HIST + tm, POOL_WIDTH), F32),
            pltpu.VMEM((tm, SB_WIDTH + POOL_WIDTH), BF16),
            pltpu.VMEM((N_PAIRS, 2 * SB_HEAD_DIM, 2 * tq), F32),
            pltpu.VMEM((N_PAIRS, 1, 2 * tq), F32),
            pltpu.VMEM((2, n_pages, SB_WIDTH, page), F32),
            pltpu.VMEM((2, n_pages, SB_WIDTH, page), F32),
            pltpu.SemaphoreType.DMA((2, 2)),
        ],
    )
    return pl.pallas_call(
        functools.partial(_even_prompt_kernel, tm=tm, tq=tq),
        grid_spec=grid_spec,
        out_shape=[
            jax.ShapeDtypeStruct(x.shape, F32),
            jax.ShapeDtypeStruct((b, SB_WIDTH, t), F32),
            jax.ShapeDtypeStruct((b, SB_WIDTH, t), F32),
            jax.ShapeDtypeStruct((b, POOL_HIST, POOL_WIDTH), F32),
            jax.ShapeDtypeStruct((rows, SB_WIDTH), F32),
        ],
        compiler_params=_params("arbitrary", "arbitrary"),
        name="even_prompt_paged_decode",
    )(sb_bias, page_table, x, mod, g_pre, g_post, w_in, pool_w, pool_scale, w_out, q_dec, cache_k, cache_v)


def _ret_log_gamma():
    return np.log(np.float32(1.0) - np.exp2(np.float32(-5.0) - np.arange(RET_HEADS, dtype=np.float32))).astype(np.float32)


def _ret_tables(chunk):
    lg = jnp.asarray(_ret_log_gamma())
    i = jnp.arange(chunk, dtype=F32)
    diff = i[:, None] - i[None, :]
    decay = jnp.where(diff[None] >= 0, jnp.exp(jnp.maximum(diff, 0.0)[None] * lg[:, None, None]), 0.0)
    cross = jnp.exp((i[None, :] + 1.0) * lg[:, None])[:, :, None]
    kdec = jnp.exp((chunk - 1.0 - i)[None, :] * lg[:, None])[:, :, None]
    full = jnp.exp(chunk * lg)
    return decay, cross, kdec, full


def _rope_tables(pos):
    half = RET_QK_DIM // 2
    inv = ROPE_BASE ** (-jnp.arange(half, dtype=F32) / half)
    ang = pos[:, None] * inv[None, :]
    return jnp.cos(ang), jnp.sin(ang)


def _rotate(x, cos, sin):
    half = RET_QK_DIM // 2
    x1, x2 = x[:, :half], x[:, half:]
    return jnp.concatenate([x1 * cos - x2 * sin, x1 * sin + x2 * cos], axis=-1)


def _group_norm_gate(o, g):
    mu = jnp.mean(o, axis=-1, keepdims=True)
    c = o - mu
    var = jnp.mean(c * c, axis=-1, keepdims=True)
    return c * lax.rsqrt(var + LN_EPS) * _silu(g)


def _odd_prompt_kernel(full_ref, x_ref, m_ref, gpre_ref, gpost_ref, cos_ref, sin_ref, win_ref, wout_ref,
                       dec_ref, crs_ref, kdc_ref, xo_ref, s_ref, obuf, *, tm):
    j = pl.program_id(1)

    @pl.when(j == 0)
    def _():
        s_ref[...] = jnp.zeros_like(s_ref)

    x = x_ref[0]
    hb = _adaln_pre(x, m_ref, gpre_ref[...]).astype(BF16)
    cos, sin = cos_ref[...], sin_ref[...]
    qrs, krs, vhs, ghs = [], [], [], []
    for hd in range(RET_HEADS):
        qo = hd * RET_QK_DIM
        ko = RET_QK_WIDTH + hd * RET_QK_DIM
        vo = 2 * RET_QK_WIDTH + hd * RET_V_DIM
        go = 2 * RET_QK_WIDTH + RET_V_WIDTH + hd * RET_V_DIM
        qrs.append(_rotate(_dot(hb, win_ref[:, qo:qo + RET_QK_DIM]), cos, sin).astype(BF16))
        krs.append(_rotate(_dot(hb, win_ref[:, ko:ko + RET_QK_DIM]), cos, sin) * (RET_QK_DIM ** -0.5))
        vhs.append(_dot(hb, win_ref[:, vo:vo + RET_V_DIM]).astype(BF16))
        ghs.append(_dot(hb, win_ref[:, go:go + RET_V_DIM]))
    for c in range(tm // RET_CHUNK):
        rows = slice(c * RET_CHUNK, (c + 1) * RET_CHUNK)
        for hd in range(RET_HEADS):
            qc = qrs[hd][rows]
            kc = krs[hd][rows]
            vc = vhs[hd][rows]
            scores = _dot_nt(qc, kc.astype(BF16)) * dec_ref[hd]
            state = s_ref[0, hd]
            o = _dot(scores.astype(BF16), vc) + _dot(qc, state.astype(BF16)) * crs_ref[hd]
            kd = (kc * kdc_ref[hd]).astype(BF16)
            s_ref[0, hd] = full_ref[hd] * state + _dot_tn(kd, vc)
            obuf[rows, hd * RET_V_DIM:(hd + 1) * RET_V_DIM] = _group_norm_gate(o, ghs[hd][rows]).astype(BF16)
    y = _dot(obuf[...], wout_ref[...])
    xo_ref[0] = x + m_ref[0, 2] * _rms(y, gpost_ref[...])


def _odd_prompt(x, mod, g_pre, g_post, w_in, w_out, tm):
    b, t, _ = x.shape
    cos, sin = _rope_tables(jnp.arange(t, dtype=F32))
    decay, cross, kdec, full = _ret_tables(RET_CHUNK)
    half = RET_QK_DIM // 2
    grid_spec = pltpu.PrefetchScalarGridSpec(
        num_scalar_prefetch=1,
        grid=(b, t // tm),
        in_specs=[
            pl.BlockSpec((1, tm, D_MODEL), lambda i, j, *_: (i, j, 0)),
            pl.BlockSpec((1, 3, 1, D_MODEL), lambda i, j, *_: (i, 0, 0, 0)),
            _resident((1, D_MODEL)),
            _resident((1, D_MODEL)),
            pl.BlockSpec((tm, half), lambda i, j, *_: (j, 0)),
            pl.BlockSpec((tm, half), lambda i, j, *_: (j, 0)),
            _resident(w_in.shape),
            _resident(w_out.shape),
            _resident(decay.shape),
            _resident(cross.shape),
            _resident(kdec.shape),
        ],
        out_specs=[
            pl.BlockSpec((1, tm, D_MODEL), lambda i, j, *_: (i, j, 0)),
            pl.BlockSpec((1, RET_HEADS, RET_QK_DIM, RET_V_DIM), lambda i, j, *_: (i, 0, 0, 0)),
        ],
        scratch_shapes=[pltpu.VMEM((tm, RET_V_WIDTH), BF16)],
    )
    return pl.pallas_call(
        functools.partial(_odd_prompt_kernel, tm=tm),
        grid_spec=grid_spec,
        out_shape=[
            jax.ShapeDtypeStruct(x.shape, F32),
            jax.ShapeDtypeStruct((b, RET_HEADS, RET_QK_DIM, RET_V_DIM), F32),
        ],
        compiler_params=_params("arbitrary", "arbitrary"),
        name="odd_prompt",
    )(full, x, mod, g_pre, g_post, cos, sin, w_in, w_out, decay, cross, kdec)


def _even_sample_pre_kernel(x_ref, m_ref, gpre_ref, win_ref, pool_ref, pw_ref, ps_ref,
                            q_ref, k_ref, v_ref, pool_out_ref, opool_ref):
    hb = _adaln_pre(x_ref[...], m_ref, gpre_ref[...]).astype(BF16)
    qkvu = _dot(hb, win_ref[...])
    q_ref[...] = qkvu[:, 0:SB_WIDTH] * (SB_HEAD_DIM ** -0.5)
    k_ref[...] = qkvu[:, SB_WIDTH:2 * SB_WIDTH].T
    v_ref[...] = qkvu[:, 2 * SB_WIDTH:3 * SB_WIDTH].T
    u = qkvu[:, 3 * SB_WIDTH:]
    nh = POOL_HIST - 1
    for gi, wdw in enumerate(POOL_WINDOWS):
        lo_, hi_ = gi * POOL_GROUP, (gi + 1) * POOL_GROUP
        ug = u[:, lo_:hi_]
        ssum = ug
        for back in range(1, wdw):
            ssum = ssum + pool_ref[nh - back, :, lo_:hi_]
        d = ssum / float(wdw) - ug
        opool_ref[:, lo_:hi_] = (_dot(d.astype(BF16), pw_ref[gi]) * ps_ref[:, lo_:hi_]).astype(BF16)
    pool_out_ref[0:nh - 1] = pool_ref[1:nh]
    pool_out_ref[nh - 1] = u


def _even_sample_pre(x, mod, g_pre, w_in, pool_hist, pool_w, pool_scale):
    rows = x.shape[0]
    outs = [
        jax.ShapeDtypeStruct((rows, SB_WIDTH), F32),
        jax.ShapeDtypeStruct((SB_WIDTH, rows), F32),
        jax.ShapeDtypeStruct((SB_WIDTH, rows), F32),
        jax.ShapeDtypeStruct(pool_hist.shape, F32),
        jax.ShapeDtypeStruct((rows, POOL_WIDTH), BF16),
    ]
    return pl.pallas_call(
        _even_sample_pre_kernel,
        out_shape=outs,
        compiler_params=pltpu.CompilerParams(vmem_limit_bytes=VMEM_LIMIT_BYTES),
        name="even_sample_pre",
    )(x, mod, g_pre, w_in, pool_hist, pool_w, pool_scale)


def _decode_sample(q, bias_ref, kpages, vpages):
    n_pages = kpages.shape[0]
    head_of_lane = lax.broadcasted_iota(jnp.int32, (SB_HEADS, SB_WIDTH), 1) // SB_HEAD_DIM
    own = head_of_lane == lax.broadcasted_iota(jnp.int32, (SB_HEADS, SB_WIDTH), 0)
    qblk = jnp.where(own, q, 0.0).astype(BF16)
    hrow = lax.broadcasted_iota(jnp.int32, (SB_HEADS, 1), 0)
    bias = jnp.zeros((SB_HEADS, 1), F32)
    for h in range(SB_HEADS):
        bias = jnp.where(hrow == h, bias_ref[h], bias)
    sps, logbs = [], []
    for p in range(n_pages):
        z = _dot(qblk, kpages[p].astype(BF16)) + bias
        sp = _softplus(z)
        sps.append(sp)
        logbs.append(z - sp)
    page = sps[0].shape[1]
    st = jnp.concatenate(sps, axis=0)
    hi = st.astype(BF16)
    lo = (st - hi.astype(F32)).astype(BF16)
    hl = jnp.concatenate([hi, lo], axis=1)
    jr = lax.broadcasted_iota(jnp.int32, (2 * page, page), 0)
    jr = jnp.where(jr >= page, jr - page, jr)
    sc = lax.broadcasted_iota(jnp.int32, (2 * page, page), 1)
    within = _dot(hl, jnp.where(jr > sc, -1.0, 0.0).astype(BF16))
    total = _dot(hl, jnp.full((2 * page, page), -1.0, BF16))
    carry = jnp.zeros((SB_HEADS, page), F32)
    o8 = jnp.zeros((SB_HEADS, SB_WIDTH), F32)
    for p in range(n_pages - 1, -1, -1):
        rows = slice(p * SB_HEADS, (p + 1) * SB_HEADS)
        w = jnp.exp(logbs[p] + within[rows] + carry).astype(BF16)
        carry = carry + total[rows]
        o8 = o8 + _dot_nt(w, vpages[p].astype(BF16))
    return jnp.sum(jnp.where(own, o8, 0.0), axis=0, keepdims=True)


def _even_sample_post_kernel(x_ref, m_ref, gpost_ref, osb_ref, opool_ref, wout_ref, xo_ref):
    y = _dot(osb_ref[...].astype(BF16), wout_ref[0:SB_WIDTH, :]) + _dot(opool_ref[...], wout_ref[SB_WIDTH:, :])
    xo_ref[...] = x_ref[...] + m_ref[0, 2] * _rms(y, gpost_ref[...])


def _even_sample_post(x, mod, g_post, o_sb, o_pool, w_out):
    return pl.pallas_call(
        _even_sample_post_kernel,
        out_shape=jax.ShapeDtypeStruct(x.shape, F32),
        compiler_params=pltpu.CompilerParams(vmem_limit_bytes=VMEM_LIMIT_BYTES),
        name="even_sample_post",
    )(x, mod, g_post, o_sb, o_pool, w_out)


def _odd_sample_pre_kernel(x_ref, m_ref, gpre_ref, cos_ref, sin_ref, win_ref, q_ref, k_ref, v_ref, g_ref):
    hb = _adaln_pre(x_ref[...], m_ref, gpre_ref[...]).astype(BF16)
    cos, sin = cos_ref[...], sin_ref[...]
    for hd in range(RET_HEADS):
        qo = hd * RET_QK_DIM
        ko = RET_QK_WIDTH + hd * RET_QK_DIM
        q_ref[:, qo:qo + RET_QK_DIM] = _rotate(_dot(hb, win_ref[:, qo:qo + RET_QK_DIM]), cos, sin)
        k_ref[:, qo:qo + RET_QK_DIM] = _rotate(_dot(hb, win_ref[:, ko:ko + RET_QK_DIM]), cos, sin) * (RET_QK_DIM ** -0.5)
    vo = 2 * RET_QK_WIDTH
    v_ref[...] = _dot(hb, win_ref[:, vo:vo + RET_V_WIDTH])
    g_ref[...] = _dot(hb, win_ref[:, vo + RET_V_WIDTH:])


def _odd_sample_pre(x, mod, g_pre, cos, sin, w_in):
    rows = x.shape[0]
    outs = [
        jax.ShapeDtypeStruct((rows, RET_QK_WIDTH), F32),
        jax.ShapeDtypeStruct((rows, RET_QK_WIDTH), F32),
        jax.ShapeDtypeStruct((rows, RET_V_WIDTH), F32),
        jax.ShapeDtypeStruct((rows, RET_V_WIDTH), F32),
    ]
    return pl.pallas_call(
        _odd_sample_pre_kernel,
        out_shape=outs,
        compiler_params=pltpu.CompilerParams(vmem_limit_bytes=VMEM_LIMIT_BYTES),
        name="odd_sample_pre",
    )(x, mod, g_pre, cos, sin, w_in)


def _retention_token(gam_ref, q, k, v, s_ref, so_ref):
    row0 = lax.broadcasted_iota(jnp.int32, (16, RET_QK_DIM), 0) == 0
    outs = []
    for hd in range(RET_HEADS):
        qh = q[:, hd * RET_QK_DIM:(hd + 1) * RET_QK_DIM]
        kh = k[:, hd * RET_QK_DIM:(hd + 1) * RET_QK_DIM]
        vh = v[:, hd * RET_V_DIM:(hd + 1) * RET_V_DIM]
        qb = qh.astype(BF16)
        vb = vh.astype(BF16).astype(F32)
        score = jnp.sum(qb.astype(F32) * kh.astype(BF16).astype(F32), axis=-1, keepdims=True)
        state = s_ref[hd]
        q16 = jnp.where(row0, jnp.broadcast_to(qh, (16, RET_QK_DIM)), 0.0).astype(BF16)
        cross = _dot(q16, state.astype(BF16))[0:1] * gam_ref[hd]
        outs.append(score.astype(BF16).astype(F32) * vb + cross)
        kcol = jnp.broadcast_to(kh, (8, RET_QK_DIM)).T[:, 0:1]
        so_ref[hd] = gam_ref[hd] * state + kcol * vh
    return jnp.concatenate(outs, axis=-1)


def _ffn_ret_kernel(gam_ref, x_ref, m_ref, gpre_ref, gpost_ref, win_ref, wout_ref, q_ref, k_ref, v_ref, s_hbm,
                    xo_ref, o_ref, so_hbm, s_in, s_out, sem_in, sem_out, *, nb):
    t = pl.program_id(0) * pl.num_programs(1) + pl.program_id(1)
    n = pl.num_programs(0) * pl.num_programs(1)
    slot = t % 2

    def load(step, sl):
        return pltpu.make_async_copy(s_hbm.at[pl.ds(step * nb, nb)], s_in.at[sl], sem_in.at[sl])

    def store(step, sl):
        return pltpu.make_async_copy(s_out.at[sl], so_hbm.at[pl.ds(step * nb, nb)], sem_out.at[sl])

    @pl.when(t == 0)
    def _():
        load(0, 0).start()

    @pl.when(t + 1 < n)
    def _():
        load(t + 1, 1 - slot).start()

    load(t, slot).wait()

    @pl.when(t >= 2)
    def _():
        store(t - 2, slot).wait()

    for i in range(nb):
        b = t * nb + i
        o_ref[pl.ds(b, 1), :] = _retention_token(gam_ref, q_ref[pl.ds(b, 1), :], k_ref[pl.ds(b, 1), :],
                                                 v_ref[pl.ds(b, 1), :], s_in.at[slot, i], s_out.at[slot, i])
    store(t, slot).start()

    _ffn_kernel(x_ref, m_ref, gpre_ref, gpost_ref, win_ref, wout_ref, xo_ref)

    @pl.when(t == n - 1)
    def _():
        @pl.when(n >= 2)
        def _():
            store(t - 1, 1 - slot).wait()
        store(t, slot).wait()


def _ffn_sublayer_with_retention_step(x, mod, g_pre, g_post, w_in, w_out, member, q, k, v, state, tm):
    g, t, _ = x.shape
    rows = q.shape[0]
    steps = g * (t // tm)
    nb = rows // steps
    assert nb * steps == rows and mod.shape[2] == 1
    gamma = jnp.exp(jnp.asarray(_ret_log_gamma()))
    state_buf = pltpu.VMEM((2, nb, RET_HEADS, RET_QK_DIM, RET_V_DIM), F32)
    grid_spec = pltpu.PrefetchScalarGridSpec(
        num_scalar_prefetch=1,
        grid=(g, t // tm),
        in_specs=[
            pl.BlockSpec((1, tm, D_MODEL), lambda b, j, *_: (b, j, 0)),
            pl.BlockSpec((1, 3, 1, D_MODEL), lambda b, j, *_: (b, 0, 0, 0)),
            _resident((1, D_MODEL)),
            _resident((1, D_MODEL)),
            _resident_member(w_in, member),
            _resident_member(w_out, member),
            _resident(q.shape),
            _resident(k.shape),
            _resident(v.shape),
            pl.BlockSpec(memory_space=pl.ANY),
        ],
        out_specs=[
            pl.BlockSpec((1, tm, D_MODEL), lambda b, j, *_: (b, j, 0)),
            pl.BlockSpec((rows, RET_V_WIDTH), lambda b, j, *_: (0, 0)),
            pl.BlockSpec(memory_space=pl.ANY),
        ],
        scratch_shapes=[state_buf, state_buf, pltpu.SemaphoreType.DMA((2,)), pltpu.SemaphoreType.DMA((2,))],
    )
    return pl.pallas_call(
        functools.partial(_ffn_ret_kernel, nb=nb),
        grid_spec=grid_spec,
        out_shape=[
            jax.ShapeDtypeStruct(x.shape, F32),
            jax.ShapeDtypeStruct((rows, RET_V_WIDTH), F32),
            jax.ShapeDtypeStruct(state.shape, F32),
        ],
        compiler_params=_params("arbitrary", "arbitrary"),
        name="ffn_sublayer_retention_step",
    )(gamma, x, mod, g_pre, g_post, w_in, w_out, q, k, v, state)


def _odd_sample_post_kernel(x_ref, m_ref, gpost_ref, o_ref, g_ref, wout_ref, xo_ref, obuf):
    for hd in range(RET_HEADS):
        cols = slice(hd * RET_V_DIM, (hd + 1) * RET_V_DIM)
        obuf[:, cols] = _group_norm_gate(o_ref[:, cols], g_ref[:, cols]).astype(BF16)
    y = _dot(obuf[...], wout_ref[...])
    xo_ref[...] = x_ref[...] + m_ref[0, 2] * _rms(y, gpost_ref[...])


def _odd_sample_post(x, mod, g_post, o, g, w_out):
    return pl.pallas_call(
        _odd_sample_post_kernel,
        out_shape=jax.ShapeDtypeStruct(x.shape, F32),
        scratch_shapes=[pltpu.VMEM((x.shape[0], RET_V_WIDTH), BF16)],
        compiler_params=pltpu.CompilerParams(vmem_limit_bytes=VMEM_LIMIT_BYTES),
        name="odd_sample_post",
    )(x, mod, g_post, o, g, w_out)


PROMPT_FFN_ROWS = 512
PROMPT_HOST_ROWS = 256
PROMPT_EVEN_ROWS = 256
PROMPT_ODD_ROWS = 512


def _trunks(xp, xs, modp, mods, cache_k, cache_v, state_pool, state_ret, page_table, norm_pre, norm_post, ffn_w_in,
            ffn_w_out, w_in_even, sb_bias, pool_w, pool_scale, w_out_even, w_in_odd, w_out_odd):
    depth = modp.shape[0]
    b = xp.shape[0]
    rows = xs.shape[0]
    page = cache_k.shape[2]
    past_len = page_table.shape[1] * page
    xs = xs.reshape(rows, D_MODEL)
    out_p = dict(k=[], v=[], pool=[], ret=[])
    out_s = dict(k=[], v=[], pool=[], ret=[])
    for l in range(depth):
        li = l // 2
        mod_p = lambda s: modp[l, 3 * s:3 * s + 3].transpose(1, 0, 2).reshape(b, 3, 1, D_MODEL)
        mod_s = lambda s: mods[l, 3 * s:3 * s + 3].reshape(1, 3, rows, D_MODEL)
        gp = lambda s: norm_pre[l, s].reshape(1, D_MODEL)
        gq = lambda s: norm_post[l, s].reshape(1, D_MODEL)
        ffn = lambda x, mod, s, half, tm: _ffn_sublayer(x, mod, gp(s), gq(s), ffn_w_in, ffn_w_out, (l, half), tm=tm)

        xp = ffn(xp, mod_p(0), 0, 0, PROMPT_FFN_ROWS)
        xs = ffn(xs[None], mod_s(0), 0, 0, rows)[0]
        if l % 2 == 0:
            pool_hist = state_pool[li].transpose(1, 0, 2)
            q, k_t, v_t, pool_new, o_pool = _even_sample_pre(xs, mod_s(1), gp(1), w_in_even[li], pool_hist, pool_w[li],
                                                             pool_scale[li].reshape(1, POOL_WIDTH))
            pages_t = lambda c: c[li].transpose(0, 2, 3, 1).reshape(-1, SB_WIDTH, page)
            xp, k, v, tail, o_sb = _even_prompt(xp, mod_p(1), gp(1), gq(1), w_in_even[li], sb_bias[li], pool_w[li],
                                                pool_scale[li].reshape(1, POOL_WIDTH), w_out_even[li],
                                                q, page_table, pages_t(cache_k), pages_t(cache_v),
                                                tm=PROMPT_EVEN_ROWS, tq=PROMPT_EVEN_ROWS)
            heads_last = lambda a: a.reshape(b, SB_HEADS, SB_HEAD_DIM, -1).transpose(0, 3, 1, 2)
            out_p["k"].append(heads_last(k))
            out_p["v"].append(heads_last(v))
            out_p["pool"].append(tail[:, 1:])
            xs = _even_sample_post(xs, mod_s(1), gq(1), o_sb, o_pool, w_out_even[li])
            sample_heads_last = lambda a: a.reshape(SB_HEADS, SB_HEAD_DIM, rows).transpose(2, 0, 1)[:, None]
            out_s["k"].append(sample_heads_last(k_t))
            out_s["v"].append(sample_heads_last(v_t))
            out_s["pool"].append(pool_new.transpose(1, 0, 2))
            xp = ffn(xp, mod_p(2), 2, 1, PROMPT_FFN_ROWS)
        else:
            xp, s = _odd_prompt(xp, mod_p(1), gp(1), gq(1), w_in_odd[li], w_out_odd[li], tm=PROMPT_ODD_ROWS)
            out_p["ret"].append(s)
            cos, sin = _rope_tables(jnp.full((1,), past_len, F32))
            q, k, v, g = _odd_sample_pre(xs, mod_s(1), gp(1), cos, sin, w_in_odd[li])
            xp, o, s_new = _ffn_sublayer_with_retention_step(xp, mod_p(2), gp(2), gq(2), ffn_w_in, ffn_w_out, (l, 1),
                                                             q, k, v, state_ret[li], tm=PROMPT_HOST_ROWS)
            xs = _odd_sample_post(xs, mod_s(1), gq(1), o, g, w_out_odd[li])
            out_s["ret"].append(s_new)
        xs = ffn(xs[None], mod_s(2), 2, 1, rows)[0]
    stacked = lambda o: tuple(jnp.stack(o[n], 0) for n in ("k", "v", "pool", "ret"))
    return (xp,) + stacked(out_p), (xs.reshape(rows, 1, D_MODEL),) + stacked(out_s)


def kernel(x_prompt, x_sample, c_prompt, c_sample, cache_k, cache_v, state_pool, state_ret, page_table, w_ada, b_ada, norm_pre, norm_post, ffn_w_in, ffn_w_out, w_in_even, sb_bias, pool_w, pool_scale, w_out_even, w_in_odd, w_out_odd):
    modp, mods = _modulation(c_prompt, c_sample, w_ada, b_ada)
    bf = lambda w: w.astype(BF16)
    (y_p, kp, vp, pp, rp), (y_s, ks, vs, ps, rs) = _trunks(
        x_prompt, x_sample, modp, mods, cache_k, cache_v, state_pool, state_ret, page_table, norm_pre, norm_post,
        bf(ffn_w_in), bf(ffn_w_out), bf(w_in_even), sb_bias, bf(pool_w), pool_scale, bf(w_out_even), bf(w_in_odd),
        bf(w_out_odd))
    return (y_p, y_s, kp, vp, pp, rp, ks, vs, ps, rs)
```

```python
import functools

import numpy as np
import jax
import jax.numpy as jnp
from jax import lax
from jax.experimental import pallas as pl
from jax.experimental.pallas import tpu as pltpu

F32 = jnp.float32
BF16 = jnp.bfloat16

D_MODEL = 1024
D_FF = 2816
N_SUB = 3
NORM_EPS = 1e-6
SB_HEADS = 8
SB_HEAD_DIM = 64
SB_WIDTH = SB_HEADS * SB_HEAD_DIM
N_PAIRS = SB_HEADS // 2
POOL_WINDOWS = (2, 4, 8, 16)
POOL_GROUP = 128
POOL_WIDTH = len(POOL_WINDOWS) * POOL_GROUP
POOL_HIST = 16
RET_HEADS = 4
RET_QK_DIM = 256
RET_V_DIM = 512
RET_QK_WIDTH = RET_HEADS * RET_QK_DIM
RET_V_WIDTH = RET_HEADS * RET_V_DIM
RET_CHUNK = 256
ROPE_BASE = 10000.0
LN_EPS = 1e-5
KBLK = 256
SUFFIX_PAD = 16
LOG2E = 1.4426950408889634
MASKED_LOGIT = -1e30
VMEM_LIMIT_BYTES = 56 * 1024 * 1024


def _params(*sem):
    return pltpu.CompilerParams(dimension_semantics=sem, vmem_limit_bytes=VMEM_LIMIT_BYTES)


def _resident(shape):
    nd = len(shape)
    return pl.BlockSpec(shape, lambda *_: (0,) * nd, pipeline_mode=pl.Buffered(1))


def _rms(x, g):
    ms = jnp.mean(x * x, axis=-1, keepdims=True)
    return x * lax.rsqrt(ms + NORM_EPS) * g


def _adaln_pre(x, m_ref, g):
    return _rms(x, g) * (1.0 + m_ref[0, 1]) + m_ref[0, 0]


def _silu(x):
    return x * jax.nn.sigmoid(x)


def _dot(a, b):
    return jnp.dot(a, b, preferred_element_type=F32)


def _dot_nt(a, b):
    return lax.dot_general(a, b, (((1,), (1,)), ((), ())), preferred_element_type=F32)


def _dot_tn(a, b):
    return lax.dot_general(a, b, (((0,), (0,)), ((), ())), preferred_element_type=F32)


def _mod_kernel(cp_ref, cs_ref, w_ref, b_ref, mp_ref, ms_ref):
    hp = _silu(cp_ref[...]).astype(BF16)
    hs = _silu(cs_ref[...]).astype(BF16)
    for m in range(3):
        w = w_ref[0, :, m * D_MODEL:(m + 1) * D_MODEL].astype(BF16)
        b = b_ref[0, m]
        mp_ref[0, m] = _dot(hp, w) + b
        ms_ref[0, m] = _dot(hs, w) + b


def _modulation(c_p, c_s, w_ada, b_ada):
    depth = w_ada.shape[0]
    nb = N_SUB * 3
    bp, bs = c_p.shape[0], c_s.shape[0]
    return pl.pallas_call(
        _mod_kernel,
        grid=(depth, N_SUB),
        in_specs=[
            pl.BlockSpec((bp, D_MODEL), lambda l, n: (0, 0)),
            pl.BlockSpec((bs, D_MODEL), lambda l, n: (0, 0)),
            pl.BlockSpec((1, D_MODEL, 3 * D_MODEL), lambda l, n: (l, 0, n)),
            pl.BlockSpec((1, 3, 1, D_MODEL), lambda l, n: (l, n, 0, 0)),
        ],
        out_specs=[
            pl.BlockSpec((1, 3, bp, D_MODEL), lambda l, n: (l, n, 0, 0)),
            pl.BlockSpec((1, 3, bs, D_MODEL), lambda l, n: (l, n, 0, 0)),
        ],
        out_shape=[
            jax.ShapeDtypeStruct((depth, nb, bp, D_MODEL), F32),
            jax.ShapeDtypeStruct((depth, nb, bs, D_MODEL), F32),
        ],
        compiler_params=_params("arbitrary", "arbitrary"),
        name="adaln_modulation",
    )(c_p, c_s, w_ada, b_ada.reshape(depth, nb, 1, D_MODEL))


def _ffn_kernel(x_ref, m_ref, gpre_ref, gpost_ref, win_ref, wout_ref, o_ref):
    x = x_ref[0]
    hb = _adaln_pre(x, m_ref, gpre_ref[...]).astype(BF16)
    gu = _dot(hb, win_ref[...])
    a = (_silu(gu[:, :D_FF]) * gu[:, D_FF:]).astype(BF16)
    y = _dot(a, wout_ref[...])
    o_ref[0] = x + 0.5 * m_ref[0, 2] * _rms(y, gpost_ref[...])


def _resident_member(stack, index):
    lead = len(index)
    shape = (None,) * lead + tuple(stack.shape[lead:])
    tail = (0,) * (stack.ndim - lead)
    return pl.BlockSpec(shape, lambda *_: tuple(index) + tail, pipeline_mode=pl.Buffered(1))


def _ffn_sublayer(x, mod, g_pre, g_post, w_in, w_out, member, tm):
    g, t, _ = x.shape
    r = mod.shape[2]
    rb = 1 if r == 1 else tm
    return pl.pallas_call(
        _ffn_kernel,
        grid=(g, t // tm),
        in_specs=[
            pl.BlockSpec((1, tm, D_MODEL), lambda b, j: (b, j, 0)),
            pl.BlockSpec((1, 3, rb, D_MODEL), (lambda b, j: (b, 0, 0, 0)) if r == 1 else (lambda b, j: (b, 0, j, 0))),
            _resident((1, D_MODEL)),
            _resident((1, D_MODEL)),
            _resident_member(w_in, member),
            _resident_member(w_out, member),
        ],
        out_specs=pl.BlockSpec((1, tm, D_MODEL), lambda b, j: (b, j, 0)),
        out_shape=jax.ShapeDtypeStruct(x.shape, F32),
        compiler_params=_params("arbitrary", "arbitrary"),
        name="ffn_sublayer",
    )(x, mod, g_pre, g_post, w_in, w_out)


def _softplus(z):
    return jnp.maximum(z, 0.0) + jnp.log(1.0 + jnp.exp(-jnp.abs(z)))


def _even_prompt_kernel(bias_ref, pt_ref, x_ref, m_ref, gpre_ref, gpost_ref, win_ref, pw_ref, ps_ref, wout_ref,
                        qd_ref, ck_hbm, cv_hbm,
                        xo_ref, k_ref, v_ref, tail_ref, od_ref,
                        kbuf, vtbuf, ubuf, obuf, acc_ref, car_ref, kd, vd, dsem, *, tm, tq):
    j = pl.program_id(1)
    nj = pl.num_programs(1)
    step = pl.program_id(0) * nj + j
    n_steps = pl.num_programs(0) * nj

    def page_copies(sample, slot):
        out = []
        for p in range(kd.shape[1]):
            pg = pt_ref[sample, p]
            out.append(pltpu.make_async_copy(ck_hbm.at[pg], kd.at[slot, p], dsem.at[0, slot]))
            out.append(pltpu.make_async_copy(cv_hbm.at[pg], vd.at[slot, p], dsem.at[1, slot]))
        return out

    def decode(sample, slot):
        for c in page_copies(sample, slot):
            c.wait()
        od_ref[pl.ds(sample, 1), :] = _decode_sample(qd_ref[pl.ds(sample, 1), :], bias_ref, kd.at[slot], vd.at[slot])

    @pl.when(step == 0)
    def _():
        for c in page_copies(0, 0):
            c.start()

    for c in page_copies(2 * step + 1, 1):
        c.start()
    decode(2 * step, 0)

    x = x_ref[0]
    hb = _adaln_pre(x, m_ref, gpre_ref[...]).astype(BF16)
    qk = _dot(hb, win_ref[:, 0:2 * SB_WIDTH])
    q = qk[:, 0:SB_WIDTH] * (SB_HEAD_DIM ** -0.5)
    k = qk[:, SB_WIDTH:]
    nsub = tm // KBLK
    for s in range(nsub):
        kbuf[j * nsub + s] = k[s * KBLK:(s + 1) * KBLK].astype(BF16)

    @pl.when(j == 0)
    def _():
        ubuf[0:POOL_HIST] = jnp.zeros((POOL_HIST, POOL_WIDTH), F32)

    def project_values():
        k_ref[0] = k.T
        vu = _dot(hb, win_ref[:, 2 * SB_WIDTH:])
        v_t = vu[:, 0:SB_WIDTH].T
        v_ref[0] = v_t
        for s in range(nsub):
            vtbuf[j * nsub + s] = v_t[:, s * KBLK:(s + 1) * KBLK].astype(BF16)
        ubuf[POOL_HIST:POOL_HIST + tm] = vu[:, SB_WIDTH:]

    lane = lax.broadcasted_iota(jnp.int32, (tq, 2 * SB_HEAD_DIM), 1)
    krow = lax.broadcasted_iota(jnp.int32, (KBLK, 2 * tq), 0)
    qcol = lax.broadcasted_iota(jnp.int32, (KBLK, 2 * tq), 1)
    qcol = jnp.where(qcol >= tq, qcol - tq, qcol)
    first_head = lax.broadcasted_iota(jnp.int32, (1, 2 * tq), 1) < tq
    mr = lax.broadcasted_iota(jnp.int32, (KBLK + SUFFIX_PAD, 2 * KBLK), 0)
    mc = lax.broadcasted_iota(jnp.int32, (KBLK + SUFFIX_PAD, 2 * KBLK), 1)
    mc = jnp.where(mc >= KBLK, mc - KBLK, mc)
    suffix_neg = jnp.where((mc > mr) | (mr == KBLK), -1.0, 0.0).astype(BF16)

    for qt in range(tm // tq):
        qtile = j * (tm // tq) + qt
        qrows = q[qt * tq:(qt + 1) * tq]
        qs_t, bias_row = [], []
        for p in range(N_PAIRS):
            q2 = qrows[:, 128 * p:128 * (p + 1)]
            qa = jnp.where(lane < SB_HEAD_DIM, q2, 0.0).T
            qb = jnp.where(lane >= SB_HEAD_DIM, q2, 0.0).T
            qs_t.append(jnp.concatenate([qa, qb], axis=1).astype(BF16))
            bias_row.append(jnp.where(first_head, bias_ref[2 * p], bias_ref[2 * p + 1]) * LOG2E)
        acc_ref[...] = jnp.zeros_like(acc_ref)
        car_ref[...] = jnp.zeros_like(car_ref)

        def attend(kb, masked, after_scores=None):
            kblk = kbuf[kb]
            zs = [_dot(kblk[:, 128 * p:128 * (p + 1)], qs_t[p]) for p in range(N_PAIRS)]
            if after_scores is not None:
                after_scores()
            if masked:
                hide = jnp.where((kb * KBLK + krow) < (qtile * tq + qcol), 0.0, MASKED_LOGIT)
            logbs, hls = [], []
            for p in range(N_PAIRS):
                z = zs[p] * LOG2E + bias_row[p]
                if masked:
                    z = z + hide
                sp = jnp.maximum(z, 0.0) + jnp.log2(1.0 + jnp.exp2(-jnp.abs(z)))
                logbs.append(z - sp)
                hi = sp.astype(BF16)
                lo = (sp - hi.astype(F32)).astype(BF16)
                hls.append(jnp.concatenate([hi, lo], axis=0))
            sums = [_dot(suffix_neg, hls[p]) for p in range(N_PAIRS)]
            ws = []
            for p in range(N_PAIRS):
                ws.append(jnp.exp2(logbs[p] + sums[p][0:KBLK] + car_ref[p]).astype(BF16))
                car_ref[p] += sums[p][KBLK:KBLK + 1]
            vtb = vtbuf[kb]
            for p in range(N_PAIRS):
                acc_ref[p] += _dot(vtb[128 * p:128 * (p + 1), :], ws[p])

        ndiag = tq // KBLK
        n_full = qtile * ndiag
        for dk in range(ndiag - 1, -1, -1):
            attend(n_full + dk, True, project_values if dk == ndiag - 1 else None)

        def body(it, carry):
            attend(n_full - 1 - it, False)
            return carry

        lax.fori_loop(0, n_full, body, 0)

        for p in range(N_PAIRS):
            a = acc_ref[p]
            o_t = jnp.concatenate([a[0:SB_HEAD_DIM, 0:tq], a[SB_HEAD_DIM:, tq:]], axis=0)
            obuf[qt * tq:(qt + 1) * tq, 128 * p:128 * (p + 1)] = o_t.T.astype(BF16)

    @pl.when(step + 1 < n_steps)
    def _():
        for c in page_copies(2 * step + 2, 0):
            c.start()

    decode(2 * step + 1, 1)

    y = _dot(obuf[:, 0:SB_WIDTH], wout_ref[0:SB_WIDTH, :])

    pos1 = (j * tm + 1 + lax.broadcasted_iota(jnp.int32, (tm, POOL_GROUP), 0)).astype(F32)
    for gi, wdw in enumerate(POOL_WINDOWS):
        lo_, hi_ = gi * POOL_GROUP, (gi + 1) * POOL_GROUP
        ug = ubuf[POOL_HIST:POOL_HIST + tm, lo_:hi_]
        ssum = ug
        for sft in range(1, wdw):
            ssum = ssum + ubuf[POOL_HIST - sft:POOL_HIST - sft + tm, lo_:hi_]
        d = ssum / jnp.minimum(pos1, float(wdw)) - ug
        yg = _dot(d.astype(BF16), pw_ref[gi]) * ps_ref[:, lo_:hi_]
        obuf[:, SB_WIDTH + lo_:SB_WIDTH + hi_] = yg.astype(BF16)
    hist = ubuf[tm:tm + POOL_HIST]
    ubuf[0:POOL_HIST] = hist

    @pl.when(j == nj - 1)
    def _():
        tail_ref[0] = hist

    y = y + _dot(obuf[:, SB_WIDTH:], wout_ref[SB_WIDTH:, :])
    xo_ref[0] = x + m_ref[0, 2] * _rms(y, gpost_ref[...])


def _even_prompt(x, mod, g_pre, g_post, w_in, sb_bias, pool_w, pool_scale, w_out, q_dec, page_table, cache_k, cache_v,
                 tm, tq):
    assert tm == tq and tq % KBLK == 0
    b, t, _ = x.shape
    nkb = t // KBLK
    rows, n_pages = page_table.shape
    page = cache_k.shape[2]
    assert rows == 2 * b * (t // tm)
    grid_spec = pltpu.PrefetchScalarGridSpec(
        num_scalar_prefetch=2,
        grid=(b, t // tm),
        in_specs=[
            pl.BlockSpec((1, tm, D_MODEL), lambda i, j, *_: (i, j, 0)),
            pl.BlockSpec((1, 3, 1, D_MODEL), lambda i, j, *_: (i, 0, 0, 0)),
            _resident((1, D_MODEL)),
            _resident((1, D_MODEL)),
            _resident(w_in.shape),
            _resident(pool_w.shape),
            _resident((1, POOL_WIDTH)),
            _resident(w_out.shape),
            _resident(q_dec.shape),
            pl.BlockSpec(memory_space=pl.ANY),
            pl.BlockSpec(memory_space=pl.ANY),
        ],
        out_specs=[
            pl.BlockSpec((1, tm, D_MODEL), lambda i, j, *_: (i, j, 0)),
            pl.BlockSpec((1, SB_WIDTH, tm), lambda i, j, *_: (i, 0, j)),
            pl.BlockSpec((1, SB_WIDTH, tm), lambda i, j, *_: (i, 0, j)),
            pl.BlockSpec((1, POOL_HIST, POOL_WIDTH), lambda i, j, *_: (i, 0, 0)),
            pl.BlockSpec((rows, SB_WIDTH), lambda i, j, *_: (0, 0)),
        ],
        scratch_shapes=[
            pltpu.VMEM((nkb, KBLK, SB_WIDTH), BF16),
            pltpu.VMEM((nkb, SB_WIDTH, KBLK), BF16),
            pltpu.VMEM((POOL_HIST + tm, POOL_WIDTH), F32),
            pltpu.VMEM((tm, SB_WIDTH + POOL_WIDTH), BF16),
            pltpu.VMEM((N_PAIRS, 2 * SB_HEAD_DIM, 2 * tq), F32),
            pltpu.VMEM((N_PAIRS, 1, 2 * tq), F32),
            pltpu.VMEM((2, n_pages, SB_WIDTH, page), F32),
            pltpu.VMEM((2, n_pages, SB_WIDTH, page), F32),
            pltpu.SemaphoreType.DMA((2, 2)),
        ],
    )
    return pl.pallas_call(
        functools.partial(_even_prompt_kernel, tm=tm, tq=tq),
        grid_spec=grid_spec,
        out_shape=[
            jax.ShapeDtypeStruct(x.shape, F32),
            jax.ShapeDtypeStruct((b, SB_WIDTH, t), F32),
            jax.ShapeDtypeStruct((b, SB_WIDTH, t), F32),
            jax.ShapeDtypeStruct((b, POOL_HIST, POOL_WIDTH), F32),
            jax.ShapeDtypeStruct((rows, SB_WIDTH), F32),
        ],
        compiler_params=_params("arbitrary", "arbitrary"),
        name="even_prompt_paged_decode",
    )(sb_bias, page_table, x, mod, g_pre, g_post, w_in, pool_w, pool_scale, w_out, q_dec, cache_k, cache_v)


def _ret_log_gamma():
    return np.log(np.float32(1.0) - np.exp2(np.float32(-5.0) - np.arange(RET_HEADS, dtype=np.float32))).astype(np.float32)


def _ret_tables(chunk):
    lg = jnp.asarray(_ret_log_gamma())
    i = jnp.arange(chunk, dtype=F32)
    diff = i[:, None] - i[None, :]
    decay = jnp.where(diff[None] >= 0, jnp.exp(jnp.maximum(diff, 0.0)[None] * lg[:, None, None]), 0.0)
    cross = jnp.exp((i[None, :] + 1.0) * lg[:, None])[:, :, None]
    kdec = jnp.exp((chunk - 1.0 - i)[None, :] * lg[:, None])[:, :, None]
    full = jnp.exp(chunk * lg)
    return decay, cross, kdec, full


def _rope_tables(pos):
    half = RET_QK_DIM // 2
    inv = ROPE_BASE ** (-jnp.arange(half, dtype=F32) / half)
    ang = pos[:, None] * inv[None, :]
    return jnp.cos(ang), jnp.sin(ang)


def _rotate(x, cos, sin):
    half = RET_QK_DIM // 2
    x1, x2 = x[:, :half], x[:, half:]
    return jnp.concatenate([x1 * cos - x2 * sin, x1 * sin + x2 * cos], axis=-1)


def _group_norm_gate(o, g):
    mu = jnp.mean(o, axis=-1, keepdims=True)
    c = o - mu
    var = jnp.mean(c * c, axis=-1, keepdims=True)
    return c * lax.rsqrt(var + LN_EPS) * _silu(g)


def _odd_prompt_kernel(full_ref, x_ref, m_ref, gpre_ref, gpost_ref, cos_ref, sin_ref, win_ref, wout_ref,
                       dec_ref, crs_ref, kdc_ref, xo_ref, s_ref, obuf, *, tm):
    j = pl.program_id(1)

    @pl.when(j == 0)
    def _():
        s_ref[...] = jnp.zeros_like(s_ref)

    x = x_ref[0]
    hb = _adaln_pre(x, m_ref, gpre_ref[...]).astype(BF16)
    cos, sin = cos_ref[...], sin_ref[...]
    qrs, krs, vhs, ghs = [], [], [], []
    for hd in range(RET_HEADS):
        qo = hd * RET_QK_DIM
        ko = RET_QK_WIDTH + hd * RET_QK_DIM
        vo = 2 * RET_QK_WIDTH + hd * RET_V_DIM
        go = 2 * RET_QK_WIDTH + RET_V_WIDTH + hd * RET_V_DIM
        qrs.append(_rotate(_dot(hb, win_ref[:, qo:qo + RET_QK_DIM]), cos, sin).astype(BF16))
        krs.append(_rotate(_dot(hb, win_ref[:, ko:ko + RET_QK_DIM]), cos, sin) * (RET_QK_DIM ** -0.5))
        vhs.append(_dot(hb, win_ref[:, vo:vo + RET_V_DIM]).astype(BF16))
        ghs.append(_dot(hb, win_ref[:, go:go + RET_V_DIM]))
    for c in range(tm // RET_CHUNK):
        rows = slice(c * RET_CHUNK, (c + 1) * RET_CHUNK)
        for hd in range(RET_HEADS):
            qc = qrs[hd][rows]
            kc = krs[hd][rows]
            vc = vhs[hd][rows]
            scores = _dot_nt(qc, kc.astype(BF16)) * dec_ref[hd]
            state = s_ref[0, hd]
            o = _dot(scores.astype(BF16), vc) + _dot(qc, state.astype(BF16)) * crs_ref[hd]
            kd = (kc * kdc_ref[hd]).astype(BF16)
            s_ref[0, hd] = full_ref[hd] * state + _dot_tn(kd, vc)
            obuf[rows, hd * RET_V_DIM:(hd + 1) * RET_V_DIM] = _group_norm_gate(o, ghs[hd][rows]).astype(BF16)
    y = _dot(obuf[...], wout_ref[...])
    xo_ref[0] = x + m_ref[0, 2] * _rms(y, gpost_ref[...])


def _odd_prompt(x, mod, g_pre, g_post, w_in, w_out, tm):
    b, t, _ = x.shape
    cos, sin = _rope_tables(jnp.arange(t, dtype=F32))
    decay, cross, kdec, full = _ret_tables(RET_CHUNK)
    half = RET_QK_DIM // 2
    grid_spec = pltpu.PrefetchScalarGridSpec(
        num_scalar_prefetch=1,
        grid=(b, t // tm),
        in_specs=[
            pl.BlockSpec((1, tm, D_MODEL), lambda i, j, *_: (i, j, 0)),
            pl.BlockSpec((1, 3, 1, D_MODEL), lambda i, j, *_: (i, 0, 0, 0)),
            _resident((1, D_MODEL)),
            _resident((1, D_MODEL)),
            pl.BlockSpec((tm, half), lambda i, j, *_: (j, 0)),
            pl.BlockSpec((tm, half), lambda i, j, *_: (j, 0)),
            _resident(w_in.shape),
            _resident(w_out.shape),
            _resident(decay.shape),
            _resident(cross.shape),
            _resident(kdec.shape),
        ],
        out_specs=[
            pl.BlockSpec((1, tm, D_MODEL), lambda i, j, *_: (i, j, 0)),
            pl.BlockSpec((1, RET_HEADS, RET_QK_DIM, RET_V_DIM), lambda i, j, *_: (i, 0, 0, 0)),
        ],
        scratch_shapes=[pltpu.VMEM((tm, RET_V_WIDTH), BF16)],
    )
    return pl.pallas_call(
        functools.partial(_odd_prompt_kernel, tm=tm),
        grid_spec=grid_spec,
        out_shape=[
            jax.ShapeDtypeStruct(x.shape, F32),
            jax.ShapeDtypeStruct((b, RET_HEADS, RET_QK_DIM, RET_V_DIM), F32),
        ],
        compiler_params=_params("arbitrary", "arbitrary"),
        name="odd_prompt",
    )(full, x, mod, g_pre, g_post, cos, sin, w_in, w_out, decay, cross, kdec)


def _even_sample_pre_kernel(x_ref, m_ref, gpre_ref, win_ref, pool_ref, pw_ref, ps_ref,
                            q_ref, k_ref, v_ref, pool_out_ref, opool_ref):
    hb = _adaln_pre(x_ref[...], m_ref, gpre_ref[...]).astype(BF16)
    qkvu = _dot(hb, win_ref[...])
    q_ref[...] = qkvu[:, 0:SB_WIDTH] * (SB_HEAD_DIM ** -0.5)
    k_ref[...] = qkvu[:, SB_WIDTH:2 * SB_WIDTH].T
    v_ref[...] = qkvu[:, 2 * SB_WIDTH:3 * SB_WIDTH].T
    u = qkvu[:, 3 * SB_WIDTH:]
    nh = POOL_HIST - 1
    for gi, wdw in enumerate(POOL_WINDOWS):
        lo_, hi_ = gi * POOL_GROUP, (gi + 1) * POOL_GROUP
        ug = u[:, lo_:hi_]
        ssum = ug
        for back in range(1, wdw):
            ssum = ssum + pool_ref[nh - back, :, lo_:hi_]
        d = ssum / float(wdw) - ug
        opool_ref[:, lo_:hi_] = (_dot(d.astype(BF16), pw_ref[gi]) * ps_ref[:, lo_:hi_]).astype(BF16)
    pool_out_ref[0:nh - 1] = pool_ref[1:nh]
    pool_out_ref[nh - 1] = u


def _even_sample_pre(x, mod, g_pre, w_in, pool_hist, pool_w, pool_scale):
    rows = x.shape[0]
    outs = [
        jax.ShapeDtypeStruct((rows, SB_WIDTH), F32),
        jax.ShapeDtypeStruct((SB_WIDTH, rows), F32),
        jax.ShapeDtypeStruct((SB_WIDTH, rows), F32),
        jax.ShapeDtypeStruct(pool_hist.shape, F32),
        jax.ShapeDtypeStruct((rows, POOL_WIDTH), BF16),
    ]
    return pl.pallas_call(
        _even_sample_pre_kernel,
        out_shape=outs,
        compiler_params=pltpu.CompilerParams(vmem_limit_bytes=VMEM_LIMIT_BYTES),
        name="even_sample_pre",
    )(x, mod, g_pre, w_in, pool_hist, pool_w, pool_scale)


def _decode_sample(q, bias_ref, kpages, vpages):
    n_pages = kpages.shape[0]
    head_of_lane = lax.broadcasted_iota(jnp.int32, (SB_HEADS, SB_WIDTH), 1) // SB_HEAD_DIM
    own = head_of_lane == lax.broadcasted_iota(jnp.int32, (SB_HEADS, SB_WIDTH), 0)
    qblk = jnp.where(own, q, 0.0).astype(BF16)
    hrow = lax.broadcasted_iota(jnp.int32, (SB_HEADS, 1), 0)
    bias = jnp.zeros((SB_HEADS, 1), F32)
    for h in range(SB_HEADS):
        bias = jnp.where(hrow == h, bias_ref[h], bias)
    sps, logbs = [], []
    for p in range(n_pages):
        z = _dot(qblk, kpages[p].astype(BF16)) + bias
        sp = _softplus(z)
        sps.append(sp)
        logbs.append(z - sp)
    page = sps[0].shape[1]
    st = jnp.concatenate(sps, axis=0)
    hi = st.astype(BF16)
    lo = (st - hi.astype(F32)).astype(BF16)
    hl = jnp.concatenate([hi, lo], axis=1)
    jr = lax.broadcasted_iota(jnp.int32, (2 * page, page), 0)
    jr = jnp.where(jr >= page, jr - page, jr)
    sc = lax.broadcasted_iota(jnp.int32, (2 * page, page), 1)
    within = _dot(hl, jnp.where(jr > sc, -1.0, 0.0).astype(BF16))
    total = _dot(hl, jnp.full((2 * page, page), -1.0, BF16))
    carry = jnp.zeros((SB_HEADS, page), F32)
    o8 = jnp.zeros((SB_HEADS, SB_WIDTH), F32)
    for p in range(n_pages - 1, -1, -1):
        rows = slice(p * SB_HEADS, (p + 1) * SB_HEADS)
        w = jnp.exp(logbs[p] + within[rows] + carry).astype(BF16)
        carry = carry + total[rows]
        o8 = o8 + _dot_nt(w, vpages[p].astype(BF16))
    return jnp.sum(jnp.where(own, o8, 0.0), axis=0, keepdims=True)


def _even_sample_post_kernel(x_ref, m_ref, gpost_ref, osb_ref, opool_ref, wout_ref, xo_ref):
    y = _dot(osb_ref[...].astype(BF16), wout_ref[0:SB_WIDTH, :]) + _dot(opool_ref[...], wout_ref[SB_WIDTH:, :])
    xo_ref[...] = x_ref[...] + m_ref[0, 2] * _rms(y, gpost_ref[...])


def _even_sample_post(x, mod, g_post, o_sb, o_pool, w_out):
    return pl.pallas_call(
        _even_sample_post_kernel,
        out_shape=jax.ShapeDtypeStruct(x.shape, F32),
        compiler_params=pltpu.CompilerParams(vmem_limit_bytes=VMEM_LIMIT_BYTES),
        name="even_sample_post",
    )(x, mod, g_post, o_sb, o_pool, w_out)


def _odd_sample_pre_kernel(x_ref, m_ref, gpre_ref, cos_ref, sin_ref, win_ref, q_ref, k_ref, v_ref, g_ref):
    hb = _adaln_pre(x_ref[...], m_ref, gpre_ref[...]).astype(BF16)
    cos, sin = cos_ref[...], sin_ref[...]
    for hd in range(RET_HEADS):
        qo = hd * RET_QK_DIM
        ko = RET_QK_WIDTH + hd * RET_QK_DIM
        q_ref[:, qo:qo + RET_QK_DIM] = _rotate(_dot(hb, win_ref[:, qo:qo + RET_QK_DIM]), cos, sin)
        k_ref[:, qo:qo + RET_QK_DIM] = _rotate(_dot(hb, win_ref[:, ko:ko + RET_QK_DIM]), cos, sin) * (RET_QK_DIM ** -0.5)
    vo = 2 * RET_QK_WIDTH
    v_ref[...] = _dot(hb, win_ref[:, vo:vo + RET_V_WIDTH])
    g_ref[...] = _dot(hb, win_ref[:, vo + RET_V_WIDTH:])


def _odd_sample_pre(x, mod, g_pre, cos, sin, w_in):
    rows = x.shape[0]
    outs = [
        jax.ShapeDtypeStruct((rows, RET_QK_WIDTH), F32),
        jax.ShapeDtypeStruct((rows, RET_QK_WIDTH), F32),
        jax.ShapeDtypeStruct((rows, RET_V_WIDTH), F32),
        jax.ShapeDtypeStruct((rows, RET_V_WIDTH), F32),
    ]
    return pl.pallas_call(
        _odd_sample_pre_kernel,
        out_shape=outs,
        compiler_params=pltpu.CompilerParams(vmem_limit_bytes=VMEM_LIMIT_BYTES),
        name="odd_sample_pre",
    )(x, mod, g_pre, cos, sin, w_in)


def _retention_token(gam_ref, q, k, v, s_ref, so_ref):
    row0 = lax.broadcasted_iota(jnp.int32, (16, RET_QK_DIM), 0) == 0
    outs = []
    for hd in range(RET_HEADS):
        qh = q[:, hd * RET_QK_DIM:(hd + 1) * RET_QK_DIM]
        kh = k[:, hd * RET_QK_DIM:(hd + 1) * RET_QK_DIM]
        vh = v[:, hd * RET_V_DIM:(hd + 1) * RET_V_DIM]
        qb = qh.astype(BF16)
        vb = vh.astype(BF16).astype(F32)
        score = jnp.sum(qb.astype(F32) * kh.astype(BF16).astype(F32), axis=-1, keepdims=True)
        state = s_ref[hd]
        q16 = jnp.where(row0, jnp.broadcast_to(qh, (16, RET_QK_DIM)), 0.0).astype(BF16)
        cross = _dot(q16, state.astype(BF16))[0:1] * gam_ref[hd]
        outs.append(score.astype(BF16).astype(F32) * vb + cross)
        kcol = jnp.broadcast_to(kh, (8, RET_QK_DIM)).T[:, 0:1]
        so_ref[hd] = gam_ref[hd] * state + kcol * vh
    return jnp.concatenate(outs, axis=-1)


def _ffn_ret_kernel(gam_ref, x_ref, m_ref, gpre_ref, gpost_ref, win_ref, wout_ref, q_ref, k_ref, v_ref, s_hbm,
                    xo_ref, o_ref, so_hbm, s_in, s_out, sem_in, sem_out, *, nb):
    t = pl.program_id(0) * pl.num_programs(1) + pl.program_id(1)
    n = pl.num_programs(0) * pl.num_programs(1)
    slot = t % 2

    def load(step, sl):
        return pltpu.make_async_copy(s_hbm.at[pl.ds(step * nb, nb)], s_in.at[sl], sem_in.at[sl])

    def store(step, sl):
        return pltpu.make_async_copy(s_out.at[sl], so_hbm.at[pl.ds(step * nb, nb)], sem_out.at[sl])

    @pl.when(t == 0)
    def _():
        load(0, 0).start()

    @pl.when(t + 1 < n)
    def _():
        load(t + 1, 1 - slot).start()

    load(t, slot).wait()

    @pl.when(t >= 2)
    def _():
        store(t - 2, slot).wait()

    for i in range(nb):
        b = t * nb + i
        o_ref[pl.ds(b, 1), :] = _retention_token(gam_ref, q_ref[pl.ds(b, 1), :], k_ref[pl.ds(b, 1), :],
                                                 v_ref[pl.ds(b, 1), :], s_in.at[slot, i], s_out.at[slot, i])
    store(t, slot).start()

    _ffn_kernel(x_ref, m_ref, gpre_ref, gpost_ref, win_ref, wout_ref, xo_ref)

    @pl.when(t == n - 1)
    def _():
        @pl.when(n >= 2)
        def _():
            store(t - 1, 1 - slot).wait()
        store(t, slot).wait()


def _ffn_sublayer_with_retention_step(x, mod, g_pre, g_post, w_in, w_out, member, q, k, v, state, tm):
    g, t, _ = x.shape
    rows = q.shape[0]
    steps = g * (t // tm)
    nb = rows // steps
    assert nb * steps == rows and mod.shape[2] == 1
    gamma = jnp.exp(jnp.asarray(_ret_log_gamma()))
    state_buf = pltpu.VMEM((2, nb, RET_HEADS, RET_QK_DIM, RET_V_DIM), F32)
    grid_spec = pltpu.PrefetchScalarGridSpec(
        num_scalar_prefetch=1,
        grid=(g, t // tm),
        in_specs=[
            pl.BlockSpec((1, tm, D_MODEL), lambda b, j, *_: (b, j, 0)),
            pl.BlockSpec((1, 3, 1, D_MODEL), lambda b, j, *_: (b, 0, 0, 0)),
            _resident((1, D_MODEL)),
            _resident((1, D_MODEL)),
            _resident_member(w_in, member),
            _resident_member(w_out, member),
            _resident(q.shape),
            _resident(k.shape),
            _resident(v.shape),
            pl.BlockSpec(memory_space=pl.ANY),
        ],
        out_specs=[
            pl.BlockSpec((1, tm, D_MODEL), lambda b, j, *_: (b, j, 0)),
            pl.BlockSpec((rows, RET_V_WIDTH), lambda b, j, *_: (0, 0)),
            pl.BlockSpec(memory_space=pl.ANY),
        ],
        scratch_shapes=[state_buf, state_buf, pltpu.SemaphoreType.DMA((2,)), pltpu.SemaphoreType.DMA((2,))],
    )
    return pl.pallas_call(
        functools.partial(_ffn_ret_kernel, nb=nb),
        grid_spec=grid_spec,
        out_shape=[
            jax.ShapeDtypeStruct(x.shape, F32),
            jax.ShapeDtypeStruct((rows, RET_V_WIDTH), F32),
            jax.ShapeDtypeStruct(state.shape, F32),
        ],
        compiler_params=_params("arbitrary", "arbitrary"),
        name="ffn_sublayer_retention_step",
    )(gamma, x, mod, g_pre, g_post, w_in, w_out, q, k, v, state)


def _odd_sample_post_kernel(x_ref, m_ref, gpost_ref, o_ref, g_ref, wout_ref, xo_ref, obuf):
    for hd in range(RET_HEADS):
        cols = slice(hd * RET_V_DIM, (hd + 1) * RET_V_DIM)
        obuf[:, cols] = _group_norm_gate(o_ref[:, cols], g_ref[:, cols]).astype(BF16)
    y = _dot(obuf[...], wout_ref[...])
    xo_ref[...] = x_ref[...] + m_ref[0, 2] * _rms(y, gpost_ref[...])


def _odd_sample_post(x, mod, g_post, o, g, w_out):
    return pl.pallas_call(
        _odd_sample_post_kernel,
        out_shape=jax.ShapeDtypeStruct(x.shape, F32),
        scratch_shapes=[pltpu.VMEM((x.shape[0], RET_V_WIDTH), BF16)],
        compiler_params=pltpu.CompilerParams(vmem_limit_bytes=VMEM_LIMIT_BYTES),
        name="odd_sample_post",
    )(x, mod, g_post, o, g, w_out)


PROMPT_FFN_ROWS = 512
PROMPT_HOST_ROWS = 256
PROMPT_EVEN_ROWS = 256
PROMPT_ODD_ROWS = 512


def _trunks(xp, xs, modp, mods, cache_k, cache_v, state_pool, state_ret, page_table, norm_pre, norm_post, ffn_w_in,
            ffn_w_out, w_in_even, sb_bias, pool_w, pool_scale, w_out_even, w_in_odd, w_out_odd):
    depth = modp.shape[0]
    b = xp.shape[0]
    rows = xs.shape[0]
    page = cache_k.shape[2]
    past_len = page_table.shape[1] * page
    xs = xs.reshape(rows, D_MODEL)
    out_p = dict(k=[], v=[], pool=[], ret=[])
    out_s = dict(k=[], v=[], pool=[], ret=[])
    for l in range(depth):
        li = l // 2
        mod_p = lambda s: modp[l, 3 * s:3 * s + 3].transpose(1, 0, 2).reshape(b, 3, 1, D_MODEL)
        mod_s = lambda s: mods[l, 3 * s:3 * s + 3].reshape(1, 3, rows, D_MODEL)
        gp = lambda s: norm_pre[l, s].reshape(1, D_MODEL)
        gq = lambda s: norm_post[l, s].reshape(1, D_MODEL)
        ffn = lambda x, mod, s, half, tm: _ffn_sublayer(x, mod, gp(s), gq(s), ffn_w_in, ffn_w_out, (l, half), tm=tm)

        xp = ffn(xp, mod_p(0), 0, 0, PROMPT_FFN_ROWS)
        xs = ffn(xs[None], mod_s(0), 0, 0, rows)[0]
        if l % 2 == 0:
            pool_hist = state_pool[li].transpose(1, 0, 2)
            q, k_t, v_t, pool_new, o_pool = _even_sample_pre(xs, mod_s(1), gp(1), w_in_even[li], pool_hist, pool_w[li],
                                                             pool_scale[li].reshape(1, POOL_WIDTH))
            pages_t = lambda c: c[li].transpose(0, 2, 3, 1).reshape(-1, SB_WIDTH, page)
            xp, k, v, tail, o_sb = _even_prompt(xp, mod_p(1), gp(1), gq(1), w_in_even[li], sb_bias[li], pool_w[li],
                                                pool_scale[li].reshape(1, POOL_WIDTH), w_out_even[li],
                                                q, page_table, pages_t(cache_k), pages_t(cache_v),
                                                tm=PROMPT_EVEN_ROWS, tq=PROMPT_EVEN_ROWS)
            heads_last = lambda a: a.reshape(b, SB_HEADS, SB_HEAD_DIM, -1).transpose(0, 3, 1, 2)
            out_p["k"].append(heads_last(k))
            out_p["v"].append(heads_last(v))
            out_p["pool"].append(tail[:, 1:])
            xs = _even_sample_post(xs, mod_s(1), gq(1), o_sb, o_pool, w_out_even[li])
            sample_heads_last = lambda a: a.reshape(SB_HEADS, SB_HEAD_DIM, rows).transpose(2, 0, 1)[:, None]
            out_s["k"].append(sample_heads_last(k_t))
            out_s["v"].append(sample_heads_last(v_t))
            out_s["pool"].append(pool_new.transpose(1, 0, 2))
            xp = ffn(xp, mod_p(2), 2, 1, PROMPT_FFN_ROWS)
        else:
            xp, s = _odd_prompt(xp, mod_p(1), gp(1), gq(1), w_in_odd[li], w_out_odd[li], tm=PROMPT_ODD_ROWS)
            out_p["ret"].append(s)
            cos, sin = _rope_tables(jnp.full((1,), past_len, F32))
            q, k, v, g = _odd_sample_pre(xs, mod_s(1), gp(1), cos, sin, w_in_odd[li])
            xp, o, s_new = _ffn_sublayer_with_retention_step(xp, mod_p(2), gp(2), gq(2), ffn_w_in, ffn_w_out, (l, 1),
                                                             q, k, v, state_ret[li], tm=PROMPT_HOST_ROWS)
            xs = _odd_sample_post(xs, mod_s(1), gq(1), o, g, w_out_odd[li])
            out_s["ret"].append(s_new)
        xs = ffn(xs[None], mod_s(2), 2, 1, rows)[0]
    stacked = lambda o: tuple(jnp.stack(o[n], 0) for n in ("k", "v", "pool", "ret"))
    return (xp,) + stacked(out_p), (xs.reshape(rows, 1, D_MODEL),) + stacked(out_s)


def kernel(x_prompt, x_sample, c_prompt, c_sample, cache_k, cache_v, state_pool, state_ret, page_table, w_ada, b_ada, norm_pre, norm_post, ffn_w_in, ffn_w_out, w_in_even, sb_bias, pool_w, pool_scale, w_out_even, w_in_odd, w_out_odd):
    modp, mods = _modulation(c_prompt, c_sample, w_ada, b_ada)
    bf = lambda w: w.astype(BF16)
    (y_p, kp, vp, pp, rp), (y_s, ks, vs, ps, rs) = _trunks(
        x_prompt, x_sample, modp, mods, cache_k, cache_v, state_pool, state_ret, page_table, norm_pre, norm_post,
        bf(ffn_w_in), bf(ffn_w_out), bf(w_in_even), sb_bias, bf(pool_w), pool_scale, bf(w_out_even), bf(w_in_odd),
        bf(w_out_odd))
    return (y_p, y_s, kp, vp, pp, rp, ks, vs, ps, rs)
```
